```python
import jax, jax.numpy as jnp
from jax import lax
import numpy as np

D_MODEL = 1024
BATCH = 8
SEQ = 8192
DEPTH = 1

GRID_W = 64
CTX_LEN = 256
CHUNK = 64
EPS = 1e-6
N_MOD = 6
HG_HEADS = 4
HG_DK = 128
HG_DV = 128
HG_KEY = HG_HEADS * HG_DK
HG_WIDTH = HG_HEADS * HG_DV
GLA_HEADS = 4
GLA_DK = 64
GLA_DV = 128
GLA_QK = GLA_HEADS * GLA_DK
GLA_V = GLA_HEADS * GLA_DV
GLA_GATE_RANK = 16
GLA_TAU = 16.0
FFN_HIDDEN = 2816
CONV_K = 3
IN_SIZES = (HG_KEY, HG_KEY, HG_KEY, HG_WIDTH, HG_WIDTH,
            GLA_QK, GLA_QK, GLA_V, GLA_V,
            GLA_GATE_RANK, GLA_GATE_RANK,
            D_MODEL, D_MODEL)
IN_WIDTH = sum(IN_SIZES)

kernel_name = 'hybrid_hgrn2_gla_convglu_dit_block'


def rmsnorm(x, g):
    xf = x.astype(jnp.float32)
    y = xf * lax.rsqrt(jnp.mean(xf * xf, axis=-1, keepdims=True) + EPS)
    return (y * g.astype(jnp.float32)).astype(x.dtype)


def modulate(h, shift, scale):
    return h * (1.0 + scale) + shift


def heads(a, n):
    b, t, _ = a.shape
    return a.reshape(b, t, n, -1).transpose(0, 2, 1, 3)


def merge_heads(a):
    b, n, t, d = a.shape
    return a.transpose(0, 2, 1, 3).reshape(b, t, n * d)


def flip_t(a):
    return a[:, :, ::-1]


def chunk_gated_scan(q, k, v, log_f, s0):
    bsz, h, t, _ = q.shape
    dv = v.shape[-1]
    n = t // CHUNK

    def blk(a):
        return a.astype(jnp.float32).reshape(bsz, h, n, CHUNK, a.shape[-1]).transpose(2, 0, 1, 3, 4)

    qc, kc, vc, gc = blk(q), blk(k), blk(v), blk(log_f)
    b = jnp.cumsum(gc, axis=3)
    b_last = b[:, :, :, -1:, :]
    q_dec = qc * jnp.exp(b)
    k_inv = kc * jnp.exp(-b)
    k_end = kc * jnp.exp(b_last - b)
    mask = jnp.tril(jnp.ones((CHUNK, CHUNK), dtype=bool))
    att = jnp.where(mask, jnp.einsum('nbhid,nbhjd->nbhij', q_dec, k_inv), 0.0)
    o_intra = jnp.einsum('nbhij,nbhjv->nbhiv', att, vc)

    def step(s, xs):
        q_n, k_n, v_n, dec_n = xs
        o_n = jnp.einsum('bhid,bhdv->bhiv', q_n, s)
        s = dec_n[:, :, 0, :, None] * s + jnp.einsum('bhjd,bhjv->bhdv', k_n, v_n)
        return s, o_n

    _, o_inter = lax.scan(step, s0.astype(jnp.float32), (q_dec, k_end, vc, jnp.exp(b_last)))
    o = (o_intra + o_inter).transpose(1, 2, 0, 3, 4).reshape(bsz, h, t, dv)
    return o.astype(v.dtype)


def final_state(k, v, log_f):
    b = jnp.cumsum(log_f.astype(jnp.float32), axis=2)
    w = jnp.exp(b[:, :, -1:, :] - b)
    return jnp.einsum('bhtd,bhtv->bhdv', k.astype(jnp.float32) * w, v.astype(jnp.float32))


def bidir_scan(q, k_fw, k_bw, v, g_fw, g_bw, s_fw, s_bw):
    o_fw = chunk_gated_scan(q, k_fw, v, g_fw, s_fw)
    o_bw = chunk_gated_scan(flip_t(q), flip_t(k_bw), flip_t(v), flip_t(g_bw), s_bw)
    return o_fw + flip_t(o_bw)


def mixer_inputs(h, w_in, lb, gk_w2, gk_b):
    z = h @ w_in
    idx = np.cumsum(IN_SIZES)[:-1].tolist()
    (hq, hf_fw, hf_bw, hi, hog, gq, gk, gv, gog, gr_fw, gr_bw, ga, gb) = jnp.split(z, idx, axis=-1)

    def hg_forget(zf, lb_d):
        f = lb_d + (1.0 - lb_d) * jax.nn.sigmoid(zf.astype(jnp.float32))
        return heads(jnp.log(f), HG_HEADS), heads((1.0 - f).astype(h.dtype), HG_HEADS)

    def gla_decay(r, w2, b2):
        return heads(jax.nn.log_sigmoid((r @ w2 + b2).astype(jnp.float32)) / GLA_TAU, GLA_HEADS)

    hg_g_fw, hg_k_fw = hg_forget(hf_fw, lb[0])
    hg_g_bw, hg_k_bw = hg_forget(hf_bw, lb[1])
    return dict(
        hg_q=heads(jax.nn.silu(hq) * (HG_DK ** -0.5), HG_HEADS),
        hg_k_fw=hg_k_fw, hg_k_bw=hg_k_bw, hg_g_fw=hg_g_fw, hg_g_bw=hg_g_bw,
        hg_v=heads(hi, HG_HEADS), hg_og=hog,
        gla_q=heads(gq * (GLA_DK ** -0.5), GLA_HEADS), gla_k=heads(gk, GLA_HEADS), gla_v=heads(gv, GLA_HEADS),
        gla_g_fw=gla_decay(gr_fw, gk_w2[0], gk_b[0]), gla_g_bw=gla_decay(gr_bw, gk_w2[1], gk_b[1]),
        gla_og=gog, gate_a=ga, gate_b=gb)


def context_states(p):
    return (final_state(p['hg_k_fw'], p['hg_v'], p['hg_g_fw']),
            final_state(flip_t(p['hg_k_bw']), flip_t(p['hg_v']), flip_t(p['hg_g_bw'])),
            final_state(p['gla_k'], p['gla_v'], p['gla_g_fw']),
            final_state(flip_t(p['gla_k']), flip_t(p['gla_v']), flip_t(p['gla_g_bw'])))


def mixer_outputs(p, states, hg_onorm_g, gla_onorm_g, w_a, w_b, w_out):
    s_hg_fw, s_hg_bw, s_gla_fw, s_gla_bw = states
    o_hg = bidir_scan(p['hg_q'], p['hg_k_fw'], p['hg_k_bw'], p['hg_v'], p['hg_g_fw'], p['hg_g_bw'], s_hg_fw, s_hg_bw)
    y_hg = merge_heads(rmsnorm(o_hg, hg_onorm_g)) * jax.nn.silu(p['hg_og'])
    o_gla = bidir_scan(p['gla_q'], p['gla_k'], p['gla_k'], p['gla_v'], p['gla_g_fw'], p['gla_g_bw'], s_gla_fw, s_gla_bw)
    y_gla = merge_heads(rmsnorm(o_gla, gla_onorm_g)) * jax.nn.silu(p['gla_og'])
    m = jax.nn.sigmoid(p['gate_a']) * (y_hg @ w_a) + jax.nn.sigmoid(p['gate_b']) * (y_gla @ w_b)
    return m @ w_out


def dwconv_grid(u, w, b):
    bsz, t, f = u.shape
    rows = t // GRID_W
    y = lax.conv_general_dilated(u.reshape(bsz, rows, GRID_W, f), w[:, :, None, :].astype(u.dtype), (1, 1), 'SAME',
                                 dimension_numbers=('NHWC', 'HWIO', 'NHWC'), feature_group_count=f)
    return y.reshape(bsz, t, f) + b


def dwconv_seq(u, w, b):
    f = u.shape[-1]
    y = lax.conv_general_dilated(u, w[CONV_K // 2][:, None, :].astype(u.dtype), (1,), 'SAME',
                                 dimension_numbers=('NWC', 'WIO', 'NWC'), feature_group_count=f)
    return y + b


def conv_glu(h, w_up, conv_w, conv_b, w_down, on_grid):
    u, v = jnp.split(h @ w_up, 2, axis=-1)
    u = dwconv_grid(u, conv_w, conv_b) if on_grid else dwconv_seq(u, conv_w, conv_b)
    return (jax.nn.gelu(u) * v) @ w_down


def setup_inputs(seed: int = 0) -> dict:
    key = jax.random.key(seed)
    ks = jax.random.split(key, 24)
    f32 = jnp.float32
    D, F = D_MODEL, FFN_HIDDEN

    def nrm(k, shape, scale):
        return jax.random.normal(k, shape, f32) * scale

    return {
        'x': nrm(ks[0], (BATCH, SEQ, D), 1.0),
        'c': nrm(ks[1], (BATCH, D), 1.0),
        'ctx': nrm(ks[2], (BATCH, CTX_LEN, D), 1.0),
        'c_ctx': nrm(ks[3], (D,), 1.0),
        'ada_w': nrm(ks[4], (DEPTH, D, N_MOD * D), D ** -0.5),
        'ada_b': nrm(ks[5], (DEPTH, N_MOD * D), 0.02),
        'mix_pre_g': 1.0 + nrm(ks[6], (DEPTH, D), 0.02),
        'mix_post_g': 1.0 + nrm(ks[7], (DEPTH, D), 0.02),
        'ffn_pre_g': 1.0 + nrm(ks[8], (DEPTH, D), 0.02),
        'ffn_post_g': 1.0 + nrm(ks[9], (DEPTH, D), 0.02),
        'w_in': nrm(ks[10], (DEPTH, D, IN_WIDTH), D ** -0.5),
        'hg_lb_logits': nrm(ks[11], (2, DEPTH + 1, HG_KEY), 0.1),
        'hg_onorm_g': 1.0 + nrm(ks[12], (DEPTH, HG_DV), 0.02),
        'gla_gk_w2': nrm(ks[13], (DEPTH, 2, GLA_GATE_RANK, GLA_QK), GLA_GATE_RANK ** -0.5),
        'gla_gk_b': nrm(ks[14], (DEPTH, 2, GLA_QK), 0.1),
        'gla_onorm_g': 1.0 + nrm(ks[15], (DEPTH, GLA_DV), 0.02),
        'w_branch_a': nrm(ks[16], (DEPTH, HG_WIDTH, D), HG_WIDTH ** -0.5),
        'w_branch_b': nrm(ks[17], (DEPTH, GLA_V, D), GLA_V ** -0.5),
        'w_out': nrm(ks[18], (DEPTH, D, D), D ** -0.5),
        'ffn_w_up': nrm(ks[19], (DEPTH, D, 2 * F), D ** -0.5),
        'ffn_conv_w': nrm(ks[20], (DEPTH, CONV_K, CONV_K, F), 1.0 / CONV_K),
        'ffn_conv_b': nrm(ks[21], (DEPTH, F), 0.02),
        'ffn_w_down': nrm(ks[22], (DEPTH, F, D), F ** -0.5),
    }


def reference(x, c, ctx, c_ctx, ada_w, ada_b, mix_pre_g, mix_post_g, ffn_pre_g, ffn_post_g, w_in, hg_lb_logits,
              hg_onorm_g, gla_gk_w2, gla_gk_b, gla_onorm_g, w_branch_a, w_branch_b, w_out, ffn_w_up, ffn_conv_w,
              ffn_conv_b, ffn_w_down):
    lb_all = jnp.cumsum(jax.nn.softmax(hg_lb_logits.astype(jnp.float32), axis=1), axis=1)
    for layer in range(DEPTH):
        last = layer == DEPTH - 1
        mod_lat = (jax.nn.silu(c) @ ada_w[layer] + ada_b[layer])[:, None, :]
        mod_ctx = jax.nn.silu(c_ctx) @ ada_w[layer] + ada_b[layer]
        sh1, sc1, gt1, sh2, sc2, gt2 = jnp.split(mod_lat, N_MOD, axis=-1)
        csh1, csc1, cgt1, csh2, csc2, cgt2 = jnp.split(mod_ctx, N_MOD, axis=-1)
        in_w = (w_in[layer], lb_all[:, layer], gla_gk_w2[layer], gla_gk_b[layer])
        out_w = (hg_onorm_g[layer], gla_onorm_g[layer], w_branch_a[layer], w_branch_b[layer], w_out[layer])

        p_ctx = mixer_inputs(modulate(rmsnorm(ctx, mix_pre_g[layer]), csh1, csc1), *in_w)
        states = context_states(p_ctx)
        p_lat = mixer_inputs(modulate(rmsnorm(x, mix_pre_g[layer]), sh1, sc1), *in_w)
        y = mixer_outputs(p_lat, states, *out_w)
        x = x + gt1 * rmsnorm(y, mix_post_g[layer])
        if not last:
            zero_states = tuple(jnp.zeros_like(s) for s in states)
            y_ctx = mixer_outputs(p_ctx, zero_states, *out_w)
            ctx = ctx + cgt1 * rmsnorm(y_ctx, mix_post_g[layer])

        h = modulate(rmsnorm(x, ffn_pre_g[layer]), sh2, sc2)
        y = conv_glu(h, ffn_w_up[layer], ffn_conv_w[layer], ffn_conv_b[layer], ffn_w_down[layer], True)
        x = x + gt2 * rmsnorm(y, ffn_post_g[layer])
        if not last:
            hc = modulate(rmsnorm(ctx, ffn_pre_g[layer]), csh2, csc2)
            yc = conv_glu(hc, ffn_w_up[layer], ffn_conv_w[layer], ffn_conv_b[layer], ffn_w_down[layer], False)
            ctx = ctx + cgt2 * rmsnorm(yc, ffn_post_g[layer])
    return x
```

```python
import functools

import jax
import jax.numpy as jnp
from jax import lax
from jax.experimental import pallas as pl
from jax.experimental.pallas import tpu as pltpu

F32 = jnp.float32
BF16 = jnp.bfloat16

EPS = 1e-6
GRID_W = 64
CHUNK = 64
N_MOD = 6
HG_HEADS = 4
HG_DK = 128
HG_DV = 128
GLA_HEADS = 4
GLA_DK = 64
GLA_DV = 128
GLA_RANK = 16
GLA_TAU = 16.0
LANES = 128
N_HEADS = HG_HEADS + GLA_HEADS
HG_KEY = HG_HEADS * HG_DK
HG_VAL = HG_HEADS * HG_DV
GLA_KEY = GLA_HEADS * GLA_DK
GLA_VAL = GLA_HEADS * GLA_DV
KEY_W = HG_KEY + GLA_KEY
VAL_W = HG_VAL + GLA_VAL
VMEM_LIMIT = 56 * 1024 * 1024

MIX_TILE = 512
FFN_TILE = 512
FFN_FC = 256
EPI_ROWS = 256


def _sigmoid(x):
    return 1.0 / (1.0 + jnp.exp(-x))


def _silu(x):
    return x * _sigmoid(x)


def _log_sigmoid(x):
    return jnp.minimum(x, 0.0) - jnp.log(1.0 + jnp.exp(-jnp.abs(x)))


def _rms(x, g):
    return x * lax.rsqrt(jnp.mean(x * x, axis=-1, keepdims=True) + EPS) * g


def _dot(a, b):
    return jnp.dot(a, b, preferred_element_type=F32)


def _dot_nt(a, b):
    return lax.dot_general(a, b, (((1,), (1,)), ((), ())), preferred_element_type=F32)


def _dot_tn(a, b):
    return lax.dot_general(a, b, (((0,), (0,)), ((), ())), preferred_element_type=F32)


def _cumsum_dot(tri, g):
    g_hi = g.astype(BF16)
    g_lo = (g - g_hi.astype(F32)).astype(BF16)
    return _dot(tri, g_hi) + _dot(tri, g_lo)


def _const_spec(shape):
    nd = len(shape)
    return pl.BlockSpec(shape, lambda *_: (0,) * nd, pipeline_mode=pl.Buffered(1))


def _head_layout(h):
    if h < HG_HEADS:
        return h * HG_DK, None, h * HG_DV
    j = h - HG_HEADS
    return HG_KEY + (j // 2) * LANES, j % 2, HG_VAL + j * GLA_DV


def _lane_mask(sel):
    lane = lax.broadcasted_iota(jnp.int32, (1, LANES), 1)
    return (lane < GLA_DK) if sel == 0 else (lane >= GLA_DK)


def _mod_kernel(layer, cc_ref, w_ref, b_ref, lg_ref, mod_ref, lb_ref):
    s = _silu(cc_ref[...])
    mod_ref[...] = _dot(s.astype(BF16), w_ref[...].astype(BF16)) + b_ref[...]
    n_slot = lg_ref.shape[1]
    lg = [lg_ref[:, j, :] for j in range(n_slot)]
    mx = functools.reduce(jnp.maximum, lg)
    ex = [jnp.exp(l - mx) for l in lg]
    lb_ref[...] = sum(ex[: layer + 1]) / sum(ex)


def _mod_call(cc, ada_w, ada_b, lb_logits, layer):
    d = cc.shape[1]
    n = ada_w.shape[1]
    bn = n // 4
    return pl.pallas_call(
        functools.partial(_mod_kernel, layer),
        grid=(n // bn,),
        in_specs=[
            pl.BlockSpec(cc.shape, lambda j: (0, 0)),
            pl.BlockSpec((d, bn), lambda j: (0, j)),
            pl.BlockSpec((1, bn), lambda j: (0, j)),
            pl.BlockSpec(lb_logits.shape, lambda j: (0, 0, 0)),
        ],
        out_specs=[
            pl.BlockSpec((cc.shape[0], bn), lambda j: (0, j)),
            pl.BlockSpec((lb_logits.shape[0], lb_logits.shape[2]), lambda j: (0, 0)),
        ],
        out_shape=[
            jax.ShapeDtypeStruct((cc.shape[0], n), F32),
            jax.ShapeDtypeStruct((lb_logits.shape[0], lb_logits.shape[2]), F32),
        ],
        compiler_params=pltpu.CompilerParams(dimension_semantics=("arbitrary",), vmem_limit_bytes=VMEM_LIMIT),
        name="mod_vectors",
    )(cc, ada_w, ada_b, lb_logits)


_C_HF_FW, _C_HF_BW, _C_HI, _C_GK, _C_GV, _C_R_FW, _C_R_BW, _C_END = 0, 512, 1024, 1536, 1792, 2304, 2432, 2560


def _ctx_kernel(ctx_ref, mod_ref, pre_g_ref, w_ref, lb_ref, w2_ref, b2_ref, sfw_ref, sbw_ref):
    n = ctx_ref.shape[1]
    mod = mod_ref[0]
    h = _rms(ctx_ref[0], pre_g_ref[...]) * (1.0 + mod[1:2]) + mod[0:1]
    z = _dot(h.astype(BF16), w_ref[...])
    v = jnp.concatenate([z[:, _C_HI:_C_GK], z[:, _C_GV:_C_R_FW]], axis=1).astype(BF16)
    gk = z[:, _C_GK:_C_GV]
    row = lax.broadcasted_iota(jnp.int32, (n, n), 0)
    col = lax.broadcasted_iota(jnp.int32, (n, n), 1)
    for d, (c_f, c_r, s_ref) in enumerate(((_C_HF_FW, _C_R_FW, sfw_ref), (_C_HF_BW, _C_R_BW, sbw_ref))):
        lb = lb_ref[d:d + 1, :]
        f = lb + (1.0 - lb) * _sigmoid(z[:, c_f:c_f + HG_KEY])
        r = z[:, c_r:c_r + LANES].astype(BF16)
        g_gla = _log_sigmoid(_dot(r, w2_ref[d]) + b2_ref[d]) / GLA_TAU
        g = jnp.concatenate([jnp.log(f), g_gla], axis=1)
        k = jnp.concatenate([1.0 - f, gk], axis=1)
        tri = ((col > row) if d == 0 else (col < row)).astype(BF16)
        kw = k * jnp.exp(_cumsum_dot(tri, g))
        for hd in range(N_HEADS):
            ko, sel, vo = _head_layout(hd)
            kh = kw[:, ko:ko + LANES]
            if sel is not None:
                kh = jnp.where(_lane_mask(sel), kh, 0.0)
            s_ref[0, hd] = _dot_tn(v[:, vo:vo + LANES], kh.astype(BF16))


def _ctx_call(ctx, mod3, pre_g, w_c, lb, w2p, b2):
    bsz, n, d = ctx.shape
    ctx_row = bsz
    s_shape = jax.ShapeDtypeStruct((bsz, N_HEADS, LANES, LANES), F32)
    s_spec = pl.BlockSpec((1, N_HEADS, LANES, LANES), lambda b: (b, 0, 0, 0))
    return pl.pallas_call(
        _ctx_kernel,
        grid=(bsz,),
        in_specs=[
            pl.BlockSpec((1, n, d), lambda b: (b, 0, 0)),
            pl.BlockSpec((1, N_MOD, d), lambda b: (ctx_row, 0, 0)),
            _const_spec(pre_g.shape),
            _const_spec(w_c.shape),
            _const_spec(lb.shape),
            _const_spec(w2p.shape),
            _const_spec(b2.shape),
        ],
        out_specs=[s_spec, s_spec],
        out_shape=[s_shape, s_shape],
        compiler_params=pltpu.CompilerParams(dimension_semantics=("arbitrary",), vmem_limit_bytes=VMEM_LIMIT),
        name="ctx_states",
    )(ctx, mod3, pre_g, w_c, lb, w2p, b2)


_S_Q, _S_HF, _S_GK, _S_V, _S_R, _S_END = 0, 768, 1280, 1536, 2560, 2688
_E_OG, _E_GA, _E_GB, _E_END = 0, 1024, 2048, 3072


def _mixer_kernel(reverse, final, *refs):
    if final:
        (x_ref, mod_ref, pre_g_ref, ws_ref, lb_ref, w2_ref, b2_ref, s0_ref,
         obw_ref, we_ref, onorm_ref, wa_ref, wb_ref, wo_ref, post_g_ref,
         out_ref, h_s, q_s, k_s, g_s, v_s, o_s, st_s) = refs
    else:
        (x_ref, mod_ref, pre_g_ref, ws_ref, lb_ref, w2_ref, b2_ref, s0_ref,
         out_ref, h_s, q_s, k_s, g_s, v_s, st_s) = refs
        o_s = out_ref.at[0]
    tt = x_ref.shape[1]
    n_chunks = tt // CHUNK

    @pl.when(pl.program_id(1) == 0)
    def _():
        st_s[...] = s0_ref[0]

    mod = mod_ref[0]
    h = _rms(x_ref[0], pre_g_ref[...]) * (1.0 + mod[1:2]) + mod[0:1]
    h_s[...] = h.astype(BF16)
    hb = h_s[...]
    zq = _dot(hb, ws_ref[:, _S_Q:_S_Q + HG_KEY])
    q_s[:, 0:HG_KEY] = _silu(zq) * (HG_DK ** -0.5)
    q_s[:, HG_KEY:KEY_W] = _dot(hb, ws_ref[:, _S_Q + HG_KEY:_S_HF]) * (GLA_DK ** -0.5)
    lb = lb_ref[...]
    f = lb + (1.0 - lb) * _sigmoid(_dot(hb, ws_ref[:, _S_HF:_S_GK]))
    g_s[:, 0:HG_KEY] = jnp.log(f)
    k_s[:, 0:HG_KEY] = 1.0 - f
    k_s[:, HG_KEY:KEY_W] = _dot(hb, ws_ref[:, _S_GK:_S_V])
    v_s[...] = _dot(hb, ws_ref[:, _S_V:_S_R])
    r = _dot(hb, ws_ref[:, _S_R:_S_END]).astype(BF16)
    g_s[:, HG_KEY:KEY_W] = _log_sigmoid(_dot(r, w2_ref[...]) + b2_ref[...]) / GLA_TAU

    row = lax.broadcasted_iota(jnp.int32, (CHUNK, CHUNK), 0)
    col = lax.broadcasted_iota(jnp.int32, (CHUNK, CHUNK), 1)
    causal = (col >= row) if reverse else (col <= row)
    tri = causal.astype(BF16)
    tot_row = 0 if reverse else CHUNK - 1

    def chunk_body(i, carry):
        c = (n_chunks - 1 - i) if reverse else i
        r0 = pl.multiple_of(c * CHUNK, CHUNK)
        rows = pl.ds(r0, CHUNK)
        b = _cumsum_dot(tri, g_s[rows, :])
        b_tot = b[tot_row:tot_row + 1, :]
        q_dec = q_s[rows, :] * jnp.exp(b)
        k = k_s[rows, :]
        k_inv = (k * jnp.exp(-b)).astype(BF16)
        k_end = k * jnp.exp(b_tot - b)
        dec = jnp.exp(b_tot)
        for hd in range(N_HEADS):
            ko, sel, vo = _head_layout(hd)
            qd = q_dec[:, ko:ko + LANES]
            ke = k_end[:, ko:ko + LANES]
            if sel is not None:
                msk = _lane_mask(sel)
                qd = jnp.where(msk, qd, 0.0)
                ke = jnp.where(msk, ke, 0.0)
            qd = qd.astype(BF16)
            vh = v_s[rows, vo:vo + LANES].astype(BF16)
            att = jnp.where(causal, _dot_nt(qd, k_inv[:, ko:ko + LANES]), 0.0)
            st = st_s[hd]
            o = _dot(att.astype(BF16), vh) + _dot_nt(qd, st.astype(BF16))
            o_s[rows, vo:vo + LANES] = o
            st_s[hd] = st * dec[:, ko:ko + LANES] + _dot_tn(vh, ke.astype(BF16))
        return carry

    lax.fori_loop(0, n_chunks, chunk_body, 0)

    if not final:
        return

    onorm = onorm_ref[...]
    for e in range(tt // EPI_ROWS):
        rows = pl.ds(e * EPI_ROWS, EPI_ROWS)
        hb = h_s[rows, :]
        o = o_s[rows, :] + obw_ref[0, rows, :]
        og = _dot(hb, we_ref[:, _E_OG:_E_GA])
        ys = []
        for hd in range(N_HEADS):
            lo = hd * LANES
            oh = o[:, lo:lo + LANES]
            yh = oh * lax.rsqrt(jnp.mean(oh * oh, axis=-1, keepdims=True) + EPS) * onorm[:, lo:lo + LANES]
            ys.append((yh * _silu(og[:, lo:lo + LANES])).astype(BF16))
        y_hg = jnp.concatenate(ys[:HG_HEADS], axis=1)
        y_gla = jnp.concatenate(ys[HG_HEADS:], axis=1)
        ga = _sigmoid(_dot(hb, we_ref[:, _E_GA:_E_GB]))
        gb = _sigmoid(_dot(hb, we_ref[:, _E_GB:_E_END]))
        m = ga * _dot(y_hg, wa_ref[...]) + gb * _dot(y_gla, wb_ref[...])
        y = _dot(m.astype(BF16), wo_ref[...])
        out_ref[0, rows, :] = x_ref[0, rows, :] + mod[2:3] * _rms(y, post_g_ref[...])


def _mixer_call(reverse, final, x, mod3, pre_g, w_s, lb_d, w2_d, b2_d, s0, extra=()):
    bsz, t, d = x.shape
    tt = MIX_TILE
    nt = t // tt
    if reverse:
        tile_map = lambda b, j: (b, nt - 1 - j, 0)
    else:
        tile_map = lambda b, j: (b, j, 0)
    in_specs = [
        pl.BlockSpec((1, tt, d), tile_map),
        pl.BlockSpec((1, N_MOD, d), lambda b, j: (b, 0, 0)),
        _const_spec(pre_g.shape),
        _const_spec(w_s.shape),
        _const_spec(lb_d.shape),
        _const_spec(w2_d.shape),
        _const_spec(b2_d.shape),
        pl.BlockSpec((1, N_HEADS, LANES, LANES), lambda b, j: (b, 0, 0, 0)),
    ]
    scratch = [
        pltpu.VMEM((tt, d), BF16),
        pltpu.VMEM((tt, KEY_W), F32),
        pltpu.VMEM((tt, KEY_W), F32),
        pltpu.VMEM((tt, KEY_W), F32),
        pltpu.VMEM((tt, VAL_W), F32),
    ]
    if final:
        o_bw = extra[0]
        in_specs += [pl.BlockSpec((1, tt, VAL_W), tile_map)] + [_const_spec(a.shape) for a in extra[1:]]
        scratch += [pltpu.VMEM((tt, VAL_W), F32)]
        out_w = d
    else:
        out_w = VAL_W
    scratch += [pltpu.VMEM((N_HEADS, LANES, LANES), F32)]
    return pl.pallas_call(
        functools.partial(_mixer_kernel, reverse, final),
        grid=(bsz, nt),
        in_specs=in_specs,
        out_specs=pl.BlockSpec((1, tt, out_w), tile_map),
        out_shape=jax.ShapeDtypeStruct((bsz, t, out_w), F32),
        scratch_shapes=scratch,
        compiler_params=pltpu.CompilerParams(
            dimension_semantics=("arbitrary", "arbitrary"), vmem_limit_bytes=VMEM_LIMIT),
        name="mixer_fw" if final else "mixer_bw",
    )(x, mod3, pre_g, w_s, lb_d, w2_d, b2_d, s0, *extra)


def _ffn_kernel(x_ref, xp_ref, xn_ref, mod_ref, pre_g_ref, wu_ref, wv_ref, cw_ref, cb_ref, wd_ref, post_g_ref,
                out_ref, h_s, t_s):
    tt = x_ref.shape[1]
    halo = xp_ref.shape[1]
    ext = tt + 2 * halo
    n_fc = wu_ref.shape[0]
    fc = wu_ref.shape[2]
    j = pl.program_id(1)
    mod = mod_ref[0]
    pre_g = pre_g_ref[...]

    def prep(xv):
        return (_rms(xv, pre_g) * (1.0 + mod[4:5]) + mod[3:4]).astype(BF16)

    h_s[0:halo, :] = prep(xp_ref[0])
    h_s[halo:halo + tt, :] = prep(x_ref[0])
    h_s[halo + tt:ext, :] = prep(xn_ref[0])

    erow = lax.broadcasted_iota(jnp.int32, (ext, 1), 0)
    valid = jnp.logical_and(jnp.logical_or(erow >= halo, j > 0),
                            jnp.logical_or(erow < halo + tt, j < pl.num_programs(1) - 1))
    gcol = erow % GRID_W
    has_left = gcol != 0
    has_right = gcol != GRID_W - 1

    def fc_body(c, carry):
        u = jnp.where(valid, _dot(h_s[...], wu_ref[c]), 0.0)
        v = _dot(h_s[halo:halo + tt, :], wv_ref[c])
        u_l = jnp.where(has_left, pltpu.roll(u, 1, 0), 0.0)
        u_r = jnp.where(has_right, pltpu.roll(u, ext - 1, 0), 0.0)
        cw = cw_ref[c]
        acc = jnp.zeros((tt, fc), F32) + cb_ref[c]
        for dr in range(3):
            base = halo + (dr - 1) * GRID_W
            for dc, src in enumerate((u_l, u, u_r)):
                acc = acc + src[base:base + tt, :] * cw[dr * 3 + dc:dr * 3 + dc + 1, :]
        gelu = 0.5 * acc * (1.0 + jnp.tanh(0.7978845608028654 * (acc + 0.044715 * (acc * acc * acc))))
        t_s[c] = (gelu * v).astype(BF16)
        return carry

    lax.fori_loop(0, n_fc, fc_body, 0)

    y = _dot(t_s[0], wd_ref[0])
    for c in range(1, n_fc):
        y = y + _dot(t_s[c], wd_ref[c])
    out_ref[0] = x_ref[0] + mod[5:6] * _rms(y, post_g_ref[...])


def _ffn_call(x, mod3, pre_g, wu, wv, cw, cb, wd, post_g):
    bsz, t, d = x.shape
    tt = FFN_TILE
    nt = t // tt
    halo = GRID_W
    per = tt // halo
    n_halo = t // halo
    n_fc, _, fc = wu.shape
    return pl.pallas_call(
        _ffn_kernel,
        grid=(bsz, nt),
        in_specs=[
            pl.BlockSpec((1, tt, d), lambda b, j: (b, j, 0)),
            pl.BlockSpec((1, halo, d), lambda b, j: (b, jnp.maximum(j * per - 1, 0), 0)),
            pl.BlockSpec((1, halo, d), lambda b, j: (b, jnp.minimum((j + 1) * per, n_halo - 1), 0)),
            pl.BlockSpec((1, N_MOD, d), lambda b, j: (b, 0, 0)),
            _const_spec(pre_g.shape),
            _const_spec(wu.shape),
            _const_spec(wv.shape),
            _const_spec(cw.shape),
            _const_spec(cb.shape),
            _const_spec(wd.shape),
            _const_spec(post_g.shape),
        ],
        out_specs=pl.BlockSpec((1, tt, d), lambda b, j: (b, j, 0)),
        out_shape=jax.ShapeDtypeStruct((bsz, t, d), F32),
        scratch_shapes=[
            pltpu.VMEM((tt + 2 * halo, d), BF16),
            pltpu.VMEM((n_fc, tt, fc), BF16),
        ],
        compiler_params=pltpu.CompilerParams(
            dimension_semantics=("arbitrary", "arbitrary"), vmem_limit_bytes=VMEM_LIMIT),
        name="conv_glu",
    )(x, x, x, mod3, pre_g, wu, wv, cw, cb, wd, post_g)


def _pad_cols(a, width):
    return jnp.pad(a, ((0, 0), (0, width - a.shape[1])))


def kernel(x, c, ctx, c_ctx, ada_w, ada_b, mix_pre_g, mix_post_g, ffn_pre_g, ffn_post_g, w_in, hg_lb_logits,
           hg_onorm_g, gla_gk_w2, gla_gk_b, gla_onorm_g, w_branch_a, w_branch_b, w_out, ffn_w_up, ffn_conv_w,
           ffn_conv_b, ffn_w_down):
    depth = ada_w.shape[0]
    assert depth == 1, "the context stream update between layers is not implemented"
    layer = 0
    bsz, t, d = x.shape
    assert t % MIX_TILE == 0 and t % FFN_TILE == 0 and FFN_TILE % GRID_W == 0 and MIX_TILE % CHUNK == 0

    n_rows = -(-(bsz + 1) // 8) * 8
    cc = jnp.concatenate([c, c_ctx[None, :], jnp.zeros((n_rows - bsz - 1, d), F32)], axis=0)
    mod, lb = _mod_call(cc, ada_w[layer], ada_b[layer][None, :], hg_lb_logits, layer)
    mod3 = mod.reshape(n_rows, N_MOD, d)

    sizes = (HG_KEY, HG_KEY, HG_KEY, HG_VAL, HG_VAL, GLA_KEY, GLA_KEY, GLA_VAL, GLA_VAL, GLA_RANK, GLA_RANK, d, d)
    offs = [0]
    for s in sizes:
        offs.append(offs[-1] + s)
    wl = w_in[layer].astype(BF16)
    (w_hq, w_hf_fw, w_hf_bw, w_hi, w_hog, w_gq, w_gk, w_gv, w_gog, w_r_fw, w_r_bw, w_ga, w_gb) = [
        wl[:, offs[i]:offs[i + 1]] for i in range(len(sizes))]
    w_r_fw, w_r_bw = _pad_cols(w_r_fw, LANES), _pad_cols(w_r_bw, LANES)
    w_ctx = jnp.concatenate([w_hf_fw, w_hf_bw, w_hi, w_gk, w_gv, w_r_fw, w_r_bw], axis=1)
    w_scan_fw = jnp.concatenate([w_hq, w_gq, w_hf_fw, w_gk, w_hi, w_gv, w_r_fw], axis=1)
    w_scan_bw = jnp.concatenate([w_hq, w_gq, w_hf_bw, w_gk, w_hi, w_gv, w_r_bw], axis=1)
    w_epi = jnp.concatenate([w_hog, w_gog, w_ga, w_gb], axis=1)
    w2p = jnp.pad(gla_gk_w2[layer].astype(BF16), ((0, 0), (0, LANES - GLA_RANK), (0, 0)))
    b2 = gla_gk_b[layer][:, None, :]
    pre_g = mix_pre_g[layer][None, :]
    post_g = mix_post_g[layer][None, :]
    onorm = jnp.concatenate([jnp.tile(hg_onorm_g[layer], HG_HEADS), jnp.tile(gla_onorm_g[layer], GLA_HEADS)])[None, :]

    s_fw, s_bw = _ctx_call(ctx, mod3, pre_g, w_ctx, lb, w2p, b2)

    o_bw = _mixer_call(True, False, x, mod3, pre_g, w_scan_bw, lb[1:2], w2p[1], b2[1], s_bw)
    x1 = _mixer_call(False, True, x, mod3, pre_g, w_scan_fw, lb[0:1], w2p[0], b2[0], s_fw,
                     extra=(o_bw, w_epi, onorm, w_branch_a[layer].astype(BF16), w_branch_b[layer].astype(BF16),
                            w_out[layer].astype(BF16), post_g))

    f = ffn_w_down.shape[1]
    assert f % FFN_FC == 0
    n_fc = f // FFN_FC
    w_up = ffn_w_up[layer].astype(BF16)
    wu = w_up[:, :f].reshape(d, n_fc, FFN_FC).transpose(1, 0, 2)
    wv = w_up[:, f:].reshape(d, n_fc, FFN_FC).transpose(1, 0, 2)
    wd = ffn_w_down[layer].astype(BF16).reshape(n_fc, FFN_FC, d)
    cw = ffn_conv_w[layer].reshape(9, n_fc, FFN_FC).transpose(1, 0, 2)
    cb = ffn_conv_b[layer].reshape(n_fc, 1, FFN_FC)
    return _ffn_call(x1, mod3, ffn_pre_g[layer][None, :], wu, wv, cw, cb, wd, ffn_post_g[layer][None, :])
```

```python
import functools

import jax
import jax.numpy as jnp
from jax import lax
from jax.experimental import pallas as pl
from jax.experimental.pallas import tpu as pltpu

F32 = jnp.float32
BF16 = jnp.bfloat16

EPS = 1e-6
GRID_W = 64
CHUNK = 64
N_MOD = 6
HG_HEADS = 4
HG_DK = 128
HG_DV = 128
GLA_HEADS = 4
GLA_DK = 64
GLA_DV = 128
GLA_RANK = 16
GLA_TAU = 16.0
LOG2_E = 1.4426950408889634
LANES = 128
SUBLANES = 8
N_HEADS = HG_HEADS + GLA_HEADS
HG_KEY = HG_HEADS * HG_DK
HG_VAL = HG_HEADS * HG_DV
GLA_KEY = GLA_HEADS * GLA_DK
GLA_VAL = GLA_HEADS * GLA_DV
KEY_W = HG_KEY + GLA_KEY
VAL_W = HG_VAL + GLA_VAL
_P_Q, _P_GK, _P_V, _P_END = 0, KEY_W, KEY_W + GLA_KEY, KEY_W + GLA_KEY + VAL_W
VMEM_LIMIT = 56 * 1024 * 1024

MIX_TILE = 512
FFN_TILE = 512
FFN_FC = 256
EPI_ROWS = 256
SCAN_UNROLL = 8


def _sigmoid(x):
    return 1.0 / (1.0 + jnp.exp(-x))


def _silu(x):
    return x * _sigmoid(x)


def _log_sigmoid(x):
    return jnp.minimum(x, 0.0) - jnp.log(1.0 + jnp.exp(-jnp.abs(x)))


def _rms(x, g):
    return x * lax.rsqrt(jnp.mean(x * x, axis=-1, keepdims=True) + EPS) * g


def _dot(a, b):
    return jnp.dot(a, b, preferred_element_type=F32)


def _dot_nt(a, b):
    return lax.dot_general(a, b, (((1,), (1,)), ((), ())), preferred_element_type=F32)


def _dot_tn(a, b):
    return lax.dot_general(a, b, (((0,), (0,)), ((), ())), preferred_element_type=F32)


def _cumsum_dot(tri, g):
    g_hi = g.astype(BF16)
    g_lo = (g - g_hi.astype(F32)).astype(BF16)
    return _dot(tri, g_hi) + _dot(tri, g_lo)


def _const_spec(shape):
    nd = len(shape)
    return pl.BlockSpec(shape, lambda *_: (0,) * nd, pipeline_mode=pl.Buffered(1))


def _head_layout(h):
    if h < HG_HEADS:
        return h * HG_DK, None, h * HG_DV
    j = h - HG_HEADS
    return HG_KEY + (j // 2) * LANES, j % 2, HG_VAL + j * GLA_DV


def _lane_mask(sel):
    lane = lax.broadcasted_iota(jnp.int32, (1, LANES), 1)
    return (lane < GLA_DK) if sel == 0 else (lane >= GLA_DK)


def _mod_kernel(layer, cc_ref, w_ref, b_ref, lg_ref, mod_ref, lb_ref):
    s = _silu(cc_ref[...])
    mod_ref[...] = _dot(s.astype(BF16), w_ref[...].astype(BF16)) + b_ref[...]
    n_slot = lg_ref.shape[1]
    lg = [lg_ref[:, j, :] for j in range(n_slot)]
    mx = functools.reduce(jnp.maximum, lg)
    ex = [jnp.exp(l - mx) for l in lg]
    lb_ref[...] = sum(ex[: layer + 1]) / sum(ex)


def _mod_call(cc, ada_w, ada_b, lb_logits, layer):
    d = cc.shape[1]
    n = ada_w.shape[1]
    bn = n // 4
    return pl.pallas_call(
        functools.partial(_mod_kernel, layer),
        grid=(n // bn,),
        in_specs=[
            pl.BlockSpec(cc.shape, lambda j: (0, 0)),
            pl.BlockSpec((d, bn), lambda j: (0, j)),
            pl.BlockSpec((1, bn), lambda j: (0, j)),
            pl.BlockSpec(lb_logits.shape, lambda j: (0, 0, 0)),
        ],
        out_specs=[
            pl.BlockSpec((cc.shape[0], bn), lambda j: (0, j)),
            pl.BlockSpec((lb_logits.shape[0], lb_logits.shape[2]), lambda j: (0, 0)),
        ],
        out_shape=[
            jax.ShapeDtypeStruct((cc.shape[0], n), F32),
            jax.ShapeDtypeStruct((lb_logits.shape[0], lb_logits.shape[2]), F32),
        ],
        compiler_params=pltpu.CompilerParams(dimension_semantics=("arbitrary",), vmem_limit_bytes=VMEM_LIMIT),
        name="mod_vectors",
    )(cc, ada_w, ada_b, lb_logits)


_C_HF_FW, _C_HF_BW, _C_HI, _C_GK, _C_GV, _C_R_FW, _C_R_BW, _C_END = 0, 512, 1024, 1536, 1792, 2304, 2432, 2560


def _ctx_kernel(ctx_ref, mod_ref, pre_g_ref, w_ref, lb_ref, w2_ref, b2_ref, sfw_ref, sbw_ref):
    n = ctx_ref.shape[1]
    mod = mod_ref[0]
    h = _rms(ctx_ref[0], pre_g_ref[...]) * (1.0 + mod[1:2]) + mod[0:1]
    z = _dot(h.astype(BF16), w_ref[...])
    v = jnp.concatenate([z[:, _C_HI:_C_GK], z[:, _C_GV:_C_R_FW]], axis=1).astype(BF16)
    gk = z[:, _C_GK:_C_GV]
    row = lax.broadcasted_iota(jnp.int32, (n, n), 0)
    col = lax.broadcasted_iota(jnp.int32, (n, n), 1)
    for d, (c_f, c_r, s_ref) in enumerate(((_C_HF_FW, _C_R_FW, sfw_ref), (_C_HF_BW, _C_R_BW, sbw_ref))):
        lb = lb_ref[d:d + 1, :]
        f = lb + (1.0 - lb) * _sigmoid(z[:, c_f:c_f + HG_KEY])
        r = z[:, c_r:c_r + LANES].astype(BF16)
        g_gla = _log_sigmoid(_dot(r, w2_ref[d]) + b2_ref[d]) / GLA_TAU
        g = jnp.concatenate([jnp.log(f), g_gla], axis=1)
        k = jnp.concatenate([1.0 - f, gk], axis=1)
        tri = ((col > row) if d == 0 else (col < row)).astype(BF16)
        kw = k * jnp.exp(_cumsum_dot(tri, g))
        for hd in range(N_HEADS):
            ko, sel, vo = _head_layout(hd)
            kh = kw[:, ko:ko + LANES]
            if sel is not None:
                kh = jnp.where(_lane_mask(sel), kh, 0.0)
            s_ref[0, hd] = _dot_tn(kh.astype(BF16), v[:, vo:vo + LANES])


def _ctx_call(ctx, mod3, pre_g, w_c, lb, w2p, b2):
    bsz, n, d = ctx.shape
    ctx_row = bsz
    s_shape = jax.ShapeDtypeStruct((bsz, N_HEADS, LANES, LANES), F32)
    s_spec = pl.BlockSpec((1, N_HEADS, LANES, LANES), lambda b: (b, 0, 0, 0))
    return pl.pallas_call(
        _ctx_kernel,
        grid=(bsz,),
        in_specs=[
            pl.BlockSpec((1, n, d), lambda b: (b, 0, 0)),
            pl.BlockSpec((1, N_MOD, d), lambda b: (ctx_row, 0, 0)),
            _const_spec(pre_g.shape),
            _const_spec(w_c.shape),
            _const_spec(lb.shape),
            _const_spec(w2p.shape),
            _const_spec(b2.shape),
        ],
        out_specs=[s_spec, s_spec],
        out_shape=[s_shape, s_shape],
        compiler_params=pltpu.CompilerParams(dimension_semantics=("arbitrary",), vmem_limit_bytes=VMEM_LIMIT),
        name="ctx_states",
    )(ctx, mod3, pre_g, w_c, lb, w2p, b2)


_B_Q, _B_GQ, _B_GK, _B_V, _B_HF, _B_R, _B_END = 0, 512, 768, 1024, 2048, 2560, 2688
_F_HF, _F_R, _F_END = 0, 512, 640
_E_OG, _E_GA, _E_GB, _E_END = 0, 1024, 2048, 3072


def _prefix_rows(x, reverse):
    sub = lax.broadcasted_iota(jnp.int32, (SUBLANES, 1), 0)
    s = 1
    while s < SUBLANES:
        if reverse:
            x = x + jnp.where(sub < SUBLANES - s, pltpu.roll(x, SUBLANES - s, 0), 0.0)
        else:
            x = x + jnp.where(sub >= s, pltpu.roll(x, s, 0), 0.0)
        s *= 2
    return x


def _chunk_cumsum(g_ref, r0, reverse):
    n_grp = CHUNK // SUBLANES
    order = range(n_grp - 1, -1, -1) if reverse else range(n_grp)
    edge = 0 if reverse else SUBLANES - 1
    carry = None
    out = [None] * n_grp
    for i in order:
        p = _prefix_rows(g_ref[pl.ds(r0 + i * SUBLANES, SUBLANES), :], reverse)
        if carry is not None:
            p = p + carry
        carry = p[edge:edge + 1, :]
        out[i] = p
    return jnp.concatenate(out, axis=0), carry


def _mixer_kernel(reverse, final, *refs):
    if final:
        (x_ref, mod_ref, pre_g_ref, ws_ref, lb_ref, w2_ref, b2_ref, s0_ref,
         p_ref, obw_ref, we_ref, onorm_ref, wa_ref, wb_ref, wo_ref, post_g_ref,
         out_ref, h_s, hk_s, g_s, o_s, st_s) = refs
        c_hf, c_r, c_end = _F_HF, _F_R, _F_END
    else:
        (x_ref, mod_ref, pre_g_ref, ws_ref, lb_ref, w2_ref, b2_ref, s0_ref,
         out_ref, p_ref, h_s, hk_s, g_s, st_s) = refs
        o_s = out_ref.at[0]
        c_hf, c_r, c_end = _B_HF, _B_R, _B_END
    tt = x_ref.shape[1]
    n_chunks = tt // CHUNK

    @pl.when(pl.program_id(1) == 0)
    def _():
        st_s[...] = s0_ref[0]

    mod = mod_ref[0]
    h = _rms(x_ref[0], pre_g_ref[...]) * (1.0 + mod[1:2]) + mod[0:1]
    h_s[...] = h.astype(BF16)
    hb = h_s[...]
    if not final:
        p_ref[0, :, _P_Q:_P_Q + HG_KEY] = (_silu(_dot(hb, ws_ref[:, _B_Q:_B_GQ])) * (HG_DK ** -0.5)).astype(BF16)
        p_ref[0, :, _P_Q + HG_KEY:_P_GK] = (_dot(hb, ws_ref[:, _B_GQ:_B_GK]) * (GLA_DK ** -0.5)).astype(BF16)
        p_ref[0, :, _P_GK:_P_V] = _dot(hb, ws_ref[:, _B_GK:_B_V]).astype(BF16)
        p_ref[0, :, _P_V:_P_END] = _dot(hb, ws_ref[:, _B_V:_B_HF]).astype(BF16)
    lb = lb_ref[...]
    f = lb + (1.0 - lb) * _sigmoid(_dot(hb, ws_ref[:, c_hf:c_r]))
    g_s[:, 0:HG_KEY] = jnp.log2(f)
    hk_s[...] = 1.0 - f
    r = _dot(hb, ws_ref[:, c_r:c_end]).astype(BF16)
    g_s[:, HG_KEY:KEY_W] = _log_sigmoid(_dot(r, w2_ref[...]) + b2_ref[...]) * (LOG2_E / GLA_TAU)

    row = lax.broadcasted_iota(jnp.int32, (CHUNK, CHUNK), 0)
    col = lax.broadcasted_iota(jnp.int32, (CHUNK, CHUNK), 1)
    causal = (col >= row) if reverse else (col <= row)

    def chunk_body(i, carry):
        c = (n_chunks - 1 - i) if reverse else i
        r0 = pl.multiple_of(c * CHUNK, CHUNK)
        rows = pl.ds(r0, CHUNK)
        b, b_tot = _chunk_cumsum(g_s, r0, reverse)
        q = p_ref[0, rows, _P_Q:_P_GK].astype(F32)
        k = jnp.concatenate([hk_s[rows, :], p_ref[0, rows, _P_GK:_P_V].astype(F32)], axis=1)
        q_dec = (q * jnp.exp2(b)).astype(BF16)
        k_inv = (k * jnp.exp2(-b)).astype(BF16)
        dec = jnp.exp2(b_tot)
        for hd in range(N_HEADS):
            ko, sel, vo = _head_layout(hd)
            qd = q_dec[:, ko:ko + LANES]
            ki = k_inv[:, ko:ko + LANES]
            if sel is not None:
                ki = jnp.where(_lane_mask(sel), ki, jnp.zeros_like(ki))
            vh = p_ref[0, rows, _P_V + vo:_P_V + vo + LANES]
            att = jnp.where(causal, _dot_nt(qd, ki), 0.0).astype(BF16)
            st = st_s[hd]
            o_s[rows, vo:vo + LANES] = _dot(jnp.concatenate([qd, att], axis=1),
                                            jnp.concatenate([st.astype(BF16), vh], axis=0))
            dec_col = jnp.broadcast_to(dec[:, ko:ko + LANES], (LANES, LANES)).T
            st_s[hd] = dec_col * (st + _dot_tn(ki, vh))
        return carry

    lax.fori_loop(0, n_chunks, chunk_body, 0, unroll=SCAN_UNROLL)

    if not final:
        return

    onorm = onorm_ref[...]
    for e in range(tt // EPI_ROWS):
        rows = pl.ds(e * EPI_ROWS, EPI_ROWS)
        hb = h_s[rows, :]
        o = o_s[rows, :] + obw_ref[0, rows, :]
        og = _dot(hb, we_ref[:, _E_OG:_E_GA])
        ys = []
        for hd in range(N_HEADS):
            lo = hd * LANES
            oh = o[:, lo:lo + LANES]
            yh = oh * lax.rsqrt(jnp.mean(oh * oh, axis=-1, keepdims=True) + EPS) * onorm[:, lo:lo + LANES]
            ys.append((yh * _silu(og[:, lo:lo + LANES])).astype(BF16))
        y_hg = jnp.concatenate(ys[:HG_HEADS], axis=1)
        y_gla = jnp.concatenate(ys[HG_HEADS:], axis=1)
        ga = _sigmoid(_dot(hb, we_ref[:, _E_GA:_E_GB]))
        gb = _sigmoid(_dot(hb, we_ref[:, _E_GB:_E_END]))
        m = ga * _dot(y_hg, wa_ref[...]) + gb * _dot(y_gla, wb_ref[...])
        y = _dot(m.astype(BF16), wo_ref[...])
        out_ref[0, rows, :] = x_ref[0, rows, :] + mod[2:3] * _rms(y, post_g_ref[...])


def _mixer_call(reverse, final, x, mod3, pre_g, w_s, lb_d, w2_d, b2_d, s0, extra=()):
    bsz, t, d = x.shape
    tt = MIX_TILE
    nt = t // tt
    if reverse:
        tile_map = lambda b, j: (b, nt - 1 - j, 0)
    else:
        tile_map = lambda b, j: (b, j, 0)
    in_specs = [
        pl.BlockSpec((1, tt, d), tile_map),
        pl.BlockSpec((1, N_MOD, d), lambda b, j: (b, 0, 0)),
        _const_spec(pre_g.shape),
        _const_spec(w_s.shape),
        _const_spec(lb_d.shape),
        _const_spec(w2_d.shape),
        _const_spec(b2_d.shape),
        pl.BlockSpec((1, N_HEADS, LANES, LANES), lambda b, j: (b, 0, 0, 0)),
    ]
    scratch = [
        pltpu.VMEM((tt, d), BF16),
        pltpu.VMEM((tt, HG_KEY), F32),
        pltpu.VMEM((tt, KEY_W), F32),
    ]
    if final:
        in_specs += [pl.BlockSpec((1, tt, _P_END), tile_map), pl.BlockSpec((1, tt, VAL_W), tile_map)]
        in_specs += [_const_spec(a.shape) for a in extra[2:]]
        scratch += [pltpu.VMEM((tt, VAL_W), F32)]
        out_specs = pl.BlockSpec((1, tt, d), tile_map)
        out_shape = jax.ShapeDtypeStruct((bsz, t, d), F32)
    else:
        out_specs = [pl.BlockSpec((1, tt, VAL_W), tile_map), pl.BlockSpec((1, tt, _P_END), tile_map)]
        out_shape = [jax.ShapeDtypeStruct((bsz, t, VAL_W), F32), jax.ShapeDtypeStruct((bsz, t, _P_END), BF16)]
    scratch += [pltpu.VMEM((N_HEADS, LANES, LANES), F32)]
    return pl.pallas_call(
        functools.partial(_mixer_kernel, reverse, final),
        grid=(bsz, nt),
        in_specs=in_specs,
        out_specs=out_specs,
        out_shape=out_shape,
        scratch_shapes=scratch,
        compiler_params=pltpu.CompilerParams(
            dimension_semantics=("arbitrary", "arbitrary"), vmem_limit_bytes=VMEM_LIMIT),
        name="mixer_fw" if final else "mixer_bw",
    )(x, mod3, pre_g, w_s, lb_d, w2_d, b2_d, s0, *extra)


def _ffn_kernel(x_ref, xp_ref, xn_ref, mod_ref, pre_g_ref, wu_ref, wv_ref, cw_ref, cb_ref, wd_ref, post_g_ref,
                out_ref, h_s, t_s):
    tt = x_ref.shape[1]
    halo = xp_ref.shape[1]
    ext = tt + 2 * halo
    n_fc = wu_ref.shape[0]
    fc = wu_ref.shape[2]
    j = pl.program_id(1)
    mod = mod_ref[0]
    pre_g = pre_g_ref[...]

    def prep(xv):
        return _rms(xv, pre_g) * (1.0 + mod[4:5]) + mod[3:4]

    h_s[0:halo, :] = jnp.where(j > 0, prep(xp_ref[0]), 0.0).astype(BF16)
    h_s[halo:halo + tt, :] = prep(x_ref[0]).astype(BF16)
    h_s[halo + tt:ext, :] = jnp.where(j < pl.num_programs(1) - 1, prep(xn_ref[0]), 0.0).astype(BF16)

    gcol = lax.broadcasted_iota(jnp.int32, (GRID_W, 1), 0)
    has_left = gcol != 0
    has_right = gcol != GRID_W - 1

    cw_all = cw_ref[...]
    cb_all = cb_ref[...]
    for c in range(n_fc):
        u = _dot(h_s[...], wu_ref[c])
        v = _dot(h_s[halo:halo + tt, :], wv_ref[c])
        mid = [u[e * GRID_W:(e + 1) * GRID_W, :] for e in range(ext // GRID_W)]
        left = [jnp.where(has_left, pltpu.roll(m, 1, 0), 0.0) for m in mid]
        right = [jnp.where(has_right, pltpu.roll(m, GRID_W - 1, 0), 0.0) for m in mid]
        cw = cw_all[c]
        for r in range(tt // GRID_W):
            acc = None
            for dr in range(3):
                for dc, src in enumerate((left, mid, right)):
                    term = src[r + dr] * cw[dr * 3 + dc:dr * 3 + dc + 1, :]
                    acc = term + cb_all[c] if acc is None else acc + term
            inner = acc * (0.7978845608028654 + (0.7978845608028654 * 0.044715) * (acc * acc))
            half = 0.5 * acc
            rows = slice(r * GRID_W, (r + 1) * GRID_W)
            t_s[c, rows, :] = ((half + half * jnp.tanh(inner)) * v[rows, :]).astype(BF16)

    y = _dot(t_s[0], wd_ref[0])
    for c in range(1, n_fc):
        y = y + _dot(t_s[c], wd_ref[c])
    out_ref[0] = x_ref[0] + mod[5:6] * _rms(y, post_g_ref[...])


def _ffn_call(x, mod3, pre_g, wu, wv, cw, cb, wd, post_g):
    bsz, t, d = x.shape
    tt = FFN_TILE
    nt = t // tt
    halo = GRID_W
    per = tt // halo
    n_halo = t // halo
    n_fc, _, fc = wu.shape
    return pl.pallas_call(
        _ffn_kernel,
        grid=(bsz, nt),
        in_specs=[
            pl.BlockSpec((1, tt, d), lambda b, j: (b, j, 0)),
            pl.BlockSpec((1, halo, d), lambda b, j: (b, jnp.maximum(j * per - 1, 0), 0)),
            pl.BlockSpec((1, halo, d), lambda b, j: (b, jnp.minimum((j + 1) * per, n_halo - 1), 0)),
            pl.BlockSpec((1, N_MOD, d), lambda b, j: (b, 0, 0)),
            _const_spec(pre_g.shape),
            _const_spec(wu.shape),
            _const_spec(wv.shape),
            _const_spec(cw.shape),
            _const_spec(cb.shape),
            _const_spec(wd.shape),
            _const_spec(post_g.shape),
        ],
        out_specs=pl.BlockSpec((1, tt, d), lambda b, j: (b, j, 0)),
        out_shape=jax.ShapeDtypeStruct((bsz, t, d), F32),
        scratch_shapes=[
            pltpu.VMEM((tt + 2 * halo, d), BF16),
            pltpu.VMEM((n_fc, tt, fc), BF16),
        ],
        compiler_params=pltpu.CompilerParams(
            dimension_semantics=("arbitrary", "arbitrary"), vmem_limit_bytes=VMEM_LIMIT),
        name="conv_glu",
    )(x, x, x, mod3, pre_g, wu, wv, cw, cb, wd, post_g)


def _pad_cols(a, width):
    return jnp.pad(a, ((0, 0), (0, width - a.shape[1])))


def kernel(x, c, ctx, c_ctx, ada_w, ada_b, mix_pre_g, mix_post_g, ffn_pre_g, ffn_post_g, w_in, hg_lb_logits,
           hg_onorm_g, gla_gk_w2, gla_gk_b, gla_onorm_g, w_branch_a, w_branch_b, w_out, ffn_w_up, ffn_conv_w,
           ffn_conv_b, ffn_w_down):
    depth = ada_w.shape[0]
    assert depth == 1, "the context stream update between layers is not implemented"
    layer = 0
    bsz, t, d = x.shape
    assert t % MIX_TILE == 0 and t % FFN_TILE == 0 and FFN_TILE % GRID_W == 0 and MIX_TILE % CHUNK == 0
    assert (MIX_TILE // CHUNK) % SCAN_UNROLL == 0

    n_rows = -(-(bsz + 1) // SUBLANES) * SUBLANES
    cc = jnp.concatenate([c, c_ctx[None, :], jnp.zeros((n_rows - bsz - 1, d), F32)], axis=0)
    mod, lb = _mod_call(cc, ada_w[layer], ada_b[layer][None, :], hg_lb_logits, layer)
    mod3 = mod.reshape(n_rows, N_MOD, d)

    sizes = (HG_KEY, HG_KEY, HG_KEY, HG_VAL, HG_VAL, GLA_KEY, GLA_KEY, GLA_VAL, GLA_VAL, GLA_RANK, GLA_RANK, d, d)
    offs = [0]
    for s in sizes:
        offs.append(offs[-1] + s)
    wl = w_in[layer].astype(BF16)
    (w_hq, w_hf_fw, w_hf_bw, w_hi, w_hog, w_gq, w_gk, w_gv, w_gog, w_r_fw, w_r_bw, w_ga, w_gb) = [
        wl[:, offs[i]:offs[i + 1]] for i in range(len(sizes))]
    w_r_fw, w_r_bw = _pad_cols(w_r_fw, LANES), _pad_cols(w_r_bw, LANES)
    w_ctx = jnp.concatenate([w_hf_fw, w_hf_bw, w_hi, w_gk, w_gv, w_r_fw, w_r_bw], axis=1)
    w_scan_bw = jnp.concatenate([w_hq, w_gq, w_gk, w_hi, w_gv, w_hf_bw, w_r_bw], axis=1)
    w_scan_fw = jnp.concatenate([w_hf_fw, w_r_fw], axis=1)
    w_epi = jnp.concatenate([w_hog, w_gog, w_ga, w_gb], axis=1)
    w2p = jnp.pad(gla_gk_w2[layer].astype(BF16), ((0, 0), (0, LANES - GLA_RANK), (0, 0)))
    b2 = gla_gk_b[layer][:, None, :]
    pre_g = mix_pre_g[layer][None, :]
    post_g = mix_post_g[layer][None, :]
    onorm = jnp.concatenate([jnp.tile(hg_onorm_g[layer], HG_HEADS), jnp.tile(gla_onorm_g[layer], GLA_HEADS)])[None, :]

    s_fw, s_bw = _ctx_call(ctx, mod3, pre_g, w_ctx, lb, w2p, b2)

    o_bw, proj = _mixer_call(True, False, x, mod3, pre_g, w_scan_bw, lb[1:2], w2p[1], b2[1], s_bw)
    x1 = _mixer_call(False, True, x, mod3, pre_g, w_scan_fw, lb[0:1], w2p[0], b2[0], s_fw,
                     extra=(proj, o_bw, w_epi, onorm, w_branch_a[layer].astype(BF16),
                            w_branch_b[layer].astype(BF16), w_out[layer].astype(BF16), post_g))

    f = ffn_w_down.shape[1]
    assert f % FFN_FC == 0
    n_fc = f // FFN_FC
    w_up = ffn_w_up[layer].astype(BF16)
    wu = w_up[:, :f].reshape(d, n_fc, FFN_FC).transpose(1, 0, 2)
    wv = w_up[:, f:].reshape(d, n_fc, FFN_FC).transpose(1, 0, 2)
    wd = ffn_w_down[layer].astype(BF16).reshape(n_fc, FFN_FC, d)
    cw = ffn_conv_w[layer].reshape(9, n_fc, FFN_FC).transpose(1, 0, 2)
    cb = ffn_conv_b[layer].reshape(n_fc, 1, FFN_FC)
    return _ffn_call(x1, mod3, ffn_pre_g[layer][None, :], wu, wv, cw, cb, wd, ffn_post_g[layer][None, :])
```

```python
import functools

import jax
import jax.numpy as jnp
from jax import lax
from jax.experimental import pallas as pl
from jax.experimental.pallas import tpu as pltpu

F32 = jnp.float32
BF16 = jnp.bfloat16

EPS = 1e-6
GRID_W = 64
CHUNK = 64
N_MOD = 6
HG_HEADS = 4
HG_DK = 128
HG_DV = 128
GLA_HEADS = 4
GLA_DK = 64
GLA_DV = 128
GLA_RANK = 16
GLA_TAU = 16.0
LOG2_E = 1.4426950408889634
LANES = 128
SUBLANES = 8
N_HEADS = HG_HEADS + GLA_HEADS
HG_KEY = HG_HEADS * HG_DK
HG_VAL = HG_HEADS * HG_DV
GLA_KEY = GLA_HEADS * GLA_DK
GLA_VAL = GLA_HEADS * GLA_DV
KEY_W = HG_KEY + GLA_KEY
VAL_W = HG_VAL + GLA_VAL
_P_Q, _P_GK, _P_V, _P_END = 0, KEY_W, KEY_W + GLA_KEY, KEY_W + GLA_KEY + VAL_W
VMEM_LIMIT = 56 * 1024 * 1024

MIX_TILE = 256
MIX_GROUP = 2
FFN_TILE = 512
FFN_FC = 256
EPI_ROWS = 256
SIDE_COLS = 256


def _sigmoid(x):
    return 1.0 / (1.0 + jnp.exp(-x))


def _silu(x):
    return x * _sigmoid(x)


def _log_sigmoid(x):
    return jnp.minimum(x, 0.0) - jnp.log(1.0 + jnp.exp(-jnp.abs(x)))


def _rms(x, g):
    return x * lax.rsqrt(jnp.mean(x * x, axis=-1, keepdims=True) + EPS) * g


def _dot(a, b):
    return jnp.dot(a, b, preferred_element_type=F32)


def _dot_nt(a, b):
    return lax.dot_general(a, b, (((1,), (1,)), ((), ())), preferred_element_type=F32)


def _dot_tn(a, b):
    return lax.dot_general(a, b, (((0,), (0,)), ((), ())), preferred_element_type=F32)


def _cumsum_dot(tri, g):
    g_hi = g.astype(BF16)
    g_lo = (g - g_hi.astype(F32)).astype(BF16)
    return _dot(tri, g_hi) + _dot(tri, g_lo)


def _const_spec(shape):
    nd = len(shape)
    return pl.BlockSpec(shape, lambda *_: (0,) * nd, pipeline_mode=pl.Buffered(1))


def _head_layout(h):
    if h < HG_HEADS:
        return h * HG_DK, None, h * HG_DV
    j = h - HG_HEADS
    return HG_KEY + (j // 2) * LANES, j % 2, HG_VAL + j * GLA_DV


def _lane_mask(sel):
    lane = lax.broadcasted_iota(jnp.int32, (1, LANES), 1)
    return (lane < GLA_DK) if sel == 0 else (lane >= GLA_DK)


def _mod_kernel(layer, cc_ref, w_ref, b_ref, lg_ref, mod_ref, lb_ref):
    s = _silu(cc_ref[...])
    mod_ref[...] = _dot(s.astype(BF16), w_ref[...].astype(BF16)) + b_ref[...]
    n_slot = lg_ref.shape[1]
    lg = [lg_ref[:, j, :] for j in range(n_slot)]
    mx = functools.reduce(jnp.maximum, lg)
    ex = [jnp.exp(l - mx) for l in lg]
    lb_ref[...] = sum(ex[: layer + 1]) / sum(ex)


def _mod_call(cc, ada_w, ada_b, lb_logits, layer):
    d = cc.shape[1]
    n = ada_w.shape[1]
    bn = n // 4
    return pl.pallas_call(
        functools.partial(_mod_kernel, layer),
        grid=(n // bn,),
        in_specs=[
            pl.BlockSpec(cc.shape, lambda j: (0, 0)),
            pl.BlockSpec((d, bn), lambda j: (0, j)),
            pl.BlockSpec((1, bn), lambda j: (0, j)),
            pl.BlockSpec(lb_logits.shape, lambda j: (0, 0, 0)),
        ],
        out_specs=[
            pl.BlockSpec((cc.shape[0], bn), lambda j: (0, j)),
            pl.BlockSpec((lb_logits.shape[0], lb_logits.shape[2]), lambda j: (0, 0)),
        ],
        out_shape=[
            jax.ShapeDtypeStruct((cc.shape[0], n), F32),
            jax.ShapeDtypeStruct((lb_logits.shape[0], lb_logits.shape[2]), F32),
        ],
        compiler_params=pltpu.CompilerParams(dimension_semantics=("arbitrary",), vmem_limit_bytes=VMEM_LIMIT),
        name="mod_vectors",
    )(cc, ada_w, ada_b, lb_logits)


_C_HF_FW, _C_HF_BW, _C_HI, _C_GK, _C_GV, _C_R_FW, _C_R_BW, _C_END = 0, 512, 1024, 1536, 1792, 2304, 2432, 2560


def _ctx_kernel(ctx_ref, mod_ref, pre_g_ref, w_ref, lb_ref, w2_ref, b2_ref, sfw_ref, sbw_ref):
    n = ctx_ref.shape[1]
    mod = mod_ref[0]
    h = _rms(ctx_ref[0], pre_g_ref[...]) * (1.0 + mod[1:2]) + mod[0:1]
    z = _dot(h.astype(BF16), w_ref[...])
    v = jnp.concatenate([z[:, _C_HI:_C_GK], z[:, _C_GV:_C_R_FW]], axis=1).astype(BF16)
    gk = z[:, _C_GK:_C_GV]
    row = lax.broadcasted_iota(jnp.int32, (n, n), 0)
    col = lax.broadcasted_iota(jnp.int32, (n, n), 1)
    for d, (c_f, c_r, s_ref) in enumerate(((_C_HF_FW, _C_R_FW, sfw_ref), (_C_HF_BW, _C_R_BW, sbw_ref))):
        lb = lb_ref[d:d + 1, :]
        f = lb + (1.0 - lb) * _sigmoid(z[:, c_f:c_f + HG_KEY])
        r = z[:, c_r:c_r + LANES].astype(BF16)
        g_gla = _log_sigmoid(_dot(r, w2_ref[d]) + b2_ref[d]) / GLA_TAU
        g = jnp.concatenate([jnp.log(f), g_gla], axis=1)
        k = jnp.concatenate([1.0 - f, gk], axis=1)
        tri = ((col > row) if d == 0 else (col < row)).astype(BF16)
        kw = k * jnp.exp(_cumsum_dot(tri, g))
        for hd in range(N_HEADS):
            ko, sel, vo = _head_layout(hd)
            kh = kw[:, ko:ko + LANES]
            if sel is not None:
                kh = jnp.where(_lane_mask(sel), kh, 0.0)
            s_ref[0, hd] = _dot_tn(kh.astype(BF16), v[:, vo:vo + LANES])


def _ctx_call(ctx, mod3, pre_g, w_c, lb, w2p, b2):
    bsz, n, d = ctx.shape
    ctx_row = bsz
    s_shape = jax.ShapeDtypeStruct((bsz, N_HEADS, LANES, LANES), F32)
    s_spec = pl.BlockSpec((1, N_HEADS, LANES, LANES), lambda b: (b, 0, 0, 0))
    return pl.pallas_call(
        _ctx_kernel,
        grid=(bsz,),
        in_specs=[
            pl.BlockSpec((1, n, d), lambda b: (b, 0, 0)),
            pl.BlockSpec((1, N_MOD, d), lambda b: (ctx_row, 0, 0)),
            _const_spec(pre_g.shape),
            _const_spec(w_c.shape),
            _const_spec(lb.shape),
            _const_spec(w2p.shape),
            _const_spec(b2.shape),
        ],
        out_specs=[s_spec, s_spec],
        out_shape=[s_shape, s_shape],
        compiler_params=pltpu.CompilerParams(dimension_semantics=("arbitrary",), vmem_limit_bytes=VMEM_LIMIT),
        name="ctx_states",
    )(ctx, mod3, pre_g, w_c, lb, w2p, b2)


_B_Q, _B_GQ, _B_GK, _B_V, _B_HF, _B_R, _B_END = 0, 512, 768, 1024, 2048, 2560, 2688
_F_HF, _F_R, _F_END = 0, 512, 640
_E_OG, _E_GA, _E_GB, _E_END = 0, 1024, 2048, 3072


def _prefix_rows(x, reverse):
    sub = lax.broadcasted_iota(jnp.int32, (SUBLANES, 1), 0)
    s = 1
    while s < SUBLANES:
        if reverse:
            x = x + jnp.where(sub < SUBLANES - s, pltpu.roll(x, SUBLANES - s, 0), 0.0)
        else:
            x = x + jnp.where(sub >= s, pltpu.roll(x, s, 0), 0.0)
        s *= 2
    return x


def _chunk_cumsum(g_ref, r0, reverse):
    n_grp = CHUNK // SUBLANES
    order = range(n_grp - 1, -1, -1) if reverse else range(n_grp)
    edge = 0 if reverse else SUBLANES - 1
    carry = None
    out = [None] * n_grp
    for i in order:
        p = _prefix_rows(g_ref[pl.ds(r0 + i * SUBLANES, SUBLANES), :], reverse)
        if carry is not None:
            p = p + carry
        carry = p[edge:edge + 1, :]
        out[i] = p
    return jnp.concatenate(out, axis=0), carry


def _round_robin(*stages):
    active = list(stages)
    while active:
        for g in list(active):
            if next(g):
                active.remove(g)


def _scan_stage(reverse, g_ref, hk_ref, p_ref, o_ref, st_s):
    n_chunks = g_ref.shape[0] // CHUNK
    row = lax.broadcasted_iota(jnp.int32, (CHUNK, CHUNK), 0)
    col = lax.broadcasted_iota(jnp.int32, (CHUNK, CHUNK), 1)
    causal = (col >= row) if reverse else (col <= row)

    for i in range(n_chunks):
        c = (n_chunks - 1 - i) if reverse else i
        r0 = c * CHUNK
        rows = slice(r0, r0 + CHUNK)
        b, b_tot = _chunk_cumsum(g_ref, r0, reverse)
        q = p_ref[rows, _P_Q:_P_GK].astype(F32)
        k = jnp.concatenate([hk_ref[rows, :], p_ref[rows, _P_GK:_P_V].astype(F32)], axis=1)
        q_dec = (q * jnp.exp2(b)).astype(BF16)
        k_inv = (k * jnp.exp2(-b)).astype(BF16)
        dec = jnp.exp2(b_tot)
        yield False
        for hd in range(N_HEADS):
            ko, sel, vo = _head_layout(hd)
            qd = q_dec[:, ko:ko + LANES]
            ki = k_inv[:, ko:ko + LANES]
            if sel is not None:
                ki = jnp.where(_lane_mask(sel), ki, jnp.zeros_like(ki))
            vh = p_ref[rows, _P_V + vo:_P_V + vo + LANES]
            att = jnp.where(causal, _dot_nt(qd, ki), 0.0).astype(BF16)
            st = st_s[hd]
            o_ref[rows, vo:vo + LANES] = _dot(jnp.concatenate([qd, att], axis=1),
                                              jnp.concatenate([st.astype(BF16), vh], axis=0))
            dec_col = jnp.broadcast_to(dec[:, ko:ko + LANES], (LANES, LANES)).T
            st_s[hd] = dec_col * (st + _dot_tn(ki, vh))
            if hd % 4 == 3:
                yield i == n_chunks - 1 and hd == N_HEADS - 1


def _decay_pieces(hb, ws_ref, c_hf, c_r, c_end, lb_ref, w2_ref, b2_ref, hk_ref, g_ref):
    lb = lb_ref[...]
    f = lb + (1.0 - lb) * _sigmoid(_dot(hb, ws_ref[:, c_hf:c_r]))
    g_ref[:, 0:HG_KEY] = jnp.log2(f)
    hk_ref[...] = 1.0 - f
    yield False
    r = _dot(hb, ws_ref[:, c_r:c_end]).astype(BF16)
    g_ref[:, HG_KEY:KEY_W] = _log_sigmoid(_dot(r, w2_ref[...]) + b2_ref[...]) * (LOG2_E / GLA_TAU)
    yield False


def _row_views(refs, n_in, tiled_in):
    return [[r.at[gi] if (i in tiled_in or i >= n_in) else r for i, r in enumerate(refs)]
            for gi in range(MIX_GROUP)]


def _bw_kernel(*refs):
    st_s, s0_ref = refs[-1], refs[7]

    @pl.when(pl.program_id(1) == 0)
    def _():
        for gi in range(MIX_GROUP):
            st_s[gi] = s0_ref[gi, 0]

    stages = [_bw_stage(*row) for row in _row_views(refs, 8, (0, 1, 7))]
    for _ in range(3):
        _round_robin(*stages)


def _bw_stage(x_ref, mod_ref, pre_g_ref, ws_ref, lb_ref, w2_ref, b2_ref, _, o_ref, p_ref, h_s, hk_s, g_s, st_s):
    mod = mod_ref[0]
    h = _rms(x_ref[0], pre_g_ref[...]) * (1.0 + mod[1:2]) + mod[0:1]
    h_s[...] = h.astype(BF16)
    yield False
    decay = _decay_pieces(h_s[...], ws_ref, _B_HF, _B_R, _B_END, lb_ref, w2_ref, b2_ref, hk_s, g_s)
    yield next(decay)
    next(decay)
    yield True

    for c0 in range(0, _P_END, SIDE_COLS):
        z = _dot(h_s[...], ws_ref[:, c0:c0 + SIDE_COLS])
        if c0 < _P_Q + HG_KEY:
            z = _silu(z) * (HG_DK ** -0.5)
        elif c0 < _P_GK:
            z = z * (GLA_DK ** -0.5)
        p_ref[0, :, c0:c0 + SIDE_COLS] = z.astype(BF16)
        yield c0 + SIDE_COLS == _P_END

    yield from _scan_stage(True, g_s, hk_s, p_ref.at[0], o_ref.at[0], st_s)


def _bw_call(x, mod3, pre_g, w_s, lb_d, w2_d, b2_d, s0):
    grp, nb, t, d = x.shape
    tt = MIX_TILE
    nt = t // tt
    tile_map = lambda b, j: (0, b, nt - 1 - j, 0)
    consts = (pre_g, w_s, lb_d, w2_d, b2_d)
    return pl.pallas_call(
        _bw_kernel,
        grid=(nb, nt),
        in_specs=[
            pl.BlockSpec((grp, 1, tt, d), tile_map),
            pl.BlockSpec((grp, 1, N_MOD, d), lambda b, j: (0, b, 0, 0)),
            *[_const_spec(a.shape) for a in consts],
            pl.BlockSpec((grp, 1, N_HEADS, LANES, LANES), lambda b, j: (0, b, 0, 0, 0)),
        ],
        out_specs=[pl.BlockSpec((grp, 1, tt, VAL_W), tile_map), pl.BlockSpec((grp, 1, tt, _P_END), tile_map)],
        out_shape=[jax.ShapeDtypeStruct((grp, nb, t, VAL_W), F32), jax.ShapeDtypeStruct((grp, nb, t, _P_END), BF16)],
        scratch_shapes=[
            pltpu.VMEM((grp, tt, d), BF16),
            pltpu.VMEM((grp, tt, HG_KEY), F32),
            pltpu.VMEM((grp, tt, KEY_W), F32),
            pltpu.VMEM((grp, N_HEADS, LANES, LANES), F32),
        ],
        compiler_params=pltpu.CompilerParams(
            dimension_semantics=("arbitrary", "arbitrary"), vmem_limit_bytes=VMEM_LIMIT),
        name="mixer_bw",
    )(x, mod3, *consts, s0)


def _fw_kernel(*refs):
    st_s, s0_ref = refs[-1], refs[7]

    @pl.when(pl.program_id(1) == 0)
    def _():
        for gi in range(MIX_GROUP):
            st_s[gi] = s0_ref[gi, 0]

    stages = [_fw_stage(*row) for row in _row_views(refs, 16, (0, 1, 7, 8, 9))]
    for _ in range(3):
        _round_robin(*stages)


def _fw_stage(x_ref, mod_ref, pre_g_ref, ws_ref, lb_ref, w2_ref, b2_ref, _, p_ref, obw_ref, we_ref, onorm_ref,
              wa_ref, wb_ref, wo_ref, post_g_ref, out_ref, h_s, hk_s, g_s, o_s, st_s):
    tt = x_ref.shape[1]
    mod = mod_ref[0]
    h = _rms(x_ref[0], pre_g_ref[...]) * (1.0 + mod[1:2]) + mod[0:1]
    h_s[...] = h.astype(BF16)
    yield False
    decay = _decay_pieces(h_s[...], ws_ref, _F_HF, _F_R, _F_END, lb_ref, w2_ref, b2_ref, hk_s, g_s)
    yield next(decay)
    next(decay)
    yield True

    yield from _scan_stage(False, g_s, hk_s, p_ref.at[0], o_s, st_s)

    onorm = onorm_ref[...]
    for e in range(tt // EPI_ROWS):
        rows = slice(e * EPI_ROWS, (e + 1) * EPI_ROWS)
        hb = h_s[rows, :]
        o = o_s[rows, :] + obw_ref[0, rows, :]
        og = _dot(hb, we_ref[:, _E_OG:_E_GA])
        ys = []
        for hd in range(N_HEADS):
            lo = hd * LANES
            oh = o[:, lo:lo + LANES]
            yh = oh * lax.rsqrt(jnp.mean(oh * oh, axis=-1, keepdims=True) + EPS) * onorm[:, lo:lo + LANES]
            ys.append((yh * _silu(og[:, lo:lo + LANES])).astype(BF16))
        y_hg = jnp.concatenate(ys[:HG_HEADS], axis=1)
        y_gla = jnp.concatenate(ys[HG_HEADS:], axis=1)
        yield False
        ga = _sigmoid(_dot(hb, we_ref[:, _E_GA:_E_GB]))
        gb = _sigmoid(_dot(hb, we_ref[:, _E_GB:_E_END]))
        yield False
        m = ga * _dot(y_hg, wa_ref[...]) + gb * _dot(y_gla, wb_ref[...])
        yield False
        y = _dot(m.astype(BF16), wo_ref[...])
        out_ref[0, rows, :] = x_ref[0, rows, :] + mod[2:3] * _rms(y, post_g_ref[...])
        yield e == tt // EPI_ROWS - 1


def _fw_call(x, mod3, pre_g, w_s, lb_d, w2_d, b2_d, s0, proj, o_bw, w_epi, onorm, w_a, w_b, w_o, post_g):
    grp, nb, t, d = x.shape
    tt = MIX_TILE
    nt = t // tt
    tile_map = lambda b, j: (0, b, j, 0)
    consts = (pre_g, w_s, lb_d, w2_d, b2_d)
    consts2 = (w_epi, onorm, w_a, w_b, w_o, post_g)
    return pl.pallas_call(
        _fw_kernel,
        grid=(nb, nt),
        in_specs=[
            pl.BlockSpec((grp, 1, tt, d), tile_map),
            pl.BlockSpec((grp, 1, N_MOD, d), lambda b, j: (0, b, 0, 0)),
            *[_const_spec(a.shape) for a in consts],
            pl.BlockSpec((grp, 1, N_HEADS, LANES, LANES), lambda b, j: (0, b, 0, 0, 0)),
            pl.BlockSpec((grp, 1, tt, _P_END), tile_map),
            pl.BlockSpec((grp, 1, tt, VAL_W), tile_map),
            *[_const_spec(a.shape) for a in consts2],
        ],
        out_specs=pl.BlockSpec((grp, 1, tt, d), tile_map),
        out_shape=jax.ShapeDtypeStruct((grp, nb, t, d), F32),
        scratch_shapes=[
            pltpu.VMEM((grp, tt, d), BF16),
            pltpu.VMEM((grp, tt, HG_KEY), F32),
            pltpu.VMEM((grp, tt, KEY_W), F32),
            pltpu.VMEM((grp, tt, VAL_W), F32),
            pltpu.VMEM((grp, N_HEADS, LANES, LANES), F32),
        ],
        compiler_params=pltpu.CompilerParams(
            dimension_semantics=("arbitrary", "arbitrary"), vmem_limit_bytes=VMEM_LIMIT),
        name="mixer_fw",
    )(x, mod3, *consts, s0, proj, o_bw, *consts2)


def _ffn_kernel(x_ref, xp_ref, xn_ref, mod_ref, pre_g_ref, wup_ref, cw_ref, cb_ref, wd_ref, post_g_ref,
                out_ref, h_s, t_s):
    tt = x_ref.shape[1]
    halo = xp_ref.shape[1]
    ext = tt + 2 * halo
    f = wd_ref.shape[0]
    fc = FFN_FC
    n_fc = f // fc
    j = pl.program_id(1)
    mod = mod_ref[0]
    pre_g = pre_g_ref[...]

    def prep(xv):
        return _rms(xv, pre_g) * (1.0 + mod[4:5]) + mod[3:4]

    h_s[0:halo, :] = jnp.where(j > 0, prep(xp_ref[0]), 0.0).astype(BF16)
    h_s[halo:halo + tt, :] = prep(x_ref[0]).astype(BF16)
    h_s[halo + tt:ext, :] = jnp.where(j < pl.num_programs(1) - 1, prep(xn_ref[0]), 0.0).astype(BF16)

    gcol = lax.broadcasted_iota(jnp.int32, (GRID_W, 1), 0)
    has_left = gcol != 0
    has_right = gcol != GRID_W - 1

    for c in range(n_fc):
        cols = slice(c * fc, (c + 1) * fc)
        u = _dot(h_s[...], wup_ref[:, cols])
        v = _dot(h_s[halo:halo + tt, :], wup_ref[:, f + c * fc:f + (c + 1) * fc])
        mid = [u[e * GRID_W:(e + 1) * GRID_W, :] for e in range(ext // GRID_W)]
        left = [jnp.where(has_left, pltpu.roll(m, 1, 0), 0.0) for m in mid]
        right = [jnp.where(has_right, pltpu.roll(m, GRID_W - 1, 0), 0.0) for m in mid]
        cw = cw_ref[:, cols]
        cb = cb_ref[:, cols]
        for r in range(tt // GRID_W):
            acc = None
            for dr in range(3):
                for dc, src in enumerate((left, mid, right)):
                    term = src[r + dr] * cw[dr * 3 + dc:dr * 3 + dc + 1, :]
                    acc = term + cb if acc is None else acc + term
            inner = acc * (0.7978845608028654 + (0.7978845608028654 * 0.044715) * (acc * acc))
            half = 0.5 * acc
            rows = slice(r * GRID_W, (r + 1) * GRID_W)
            t_s[c, rows, :] = ((half + half * jnp.tanh(inner)) * v[rows, :]).astype(BF16)

    y = _dot(t_s[0], wd_ref[0:fc, :])
    for c in range(1, n_fc):
        y = y + _dot(t_s[c], wd_ref[c * fc:(c + 1) * fc, :])
    out_ref[0] = x_ref[0] + mod[5:6] * _rms(y, post_g_ref[...])


def _ffn_call(x, mod3, pre_g, w_up, cw, cb, wd, post_g):
    bsz, t, d = x.shape
    tt = FFN_TILE
    nt = t // tt
    halo = GRID_W
    per = tt // halo
    n_halo = t // halo
    n_fc = wd.shape[0] // FFN_FC
    return pl.pallas_call(
        _ffn_kernel,
        grid=(bsz, nt),
        in_specs=[
            pl.BlockSpec((1, tt, d), lambda b, j: (b, j, 0)),
            pl.BlockSpec((1, halo, d), lambda b, j: (b, jnp.maximum(j * per - 1, 0), 0)),
            pl.BlockSpec((1, halo, d), lambda b, j: (b, jnp.minimum((j + 1) * per, n_halo - 1), 0)),
            pl.BlockSpec((1, N_MOD, d), lambda b, j: (b, 0, 0)),
            _const_spec(pre_g.shape),
            _const_spec(w_up.shape),
            _const_spec(cw.shape),
            _const_spec(cb.shape),
            _const_spec(wd.shape),
            _const_spec(post_g.shape),
        ],
        out_specs=pl.BlockSpec((1, tt, d), lambda b, j: (b, j, 0)),
        out_shape=jax.ShapeDtypeStruct((bsz, t, d), F32),
        scratch_shapes=[
            pltpu.VMEM((tt + 2 * halo, d), BF16),
            pltpu.VMEM((n_fc, tt, FFN_FC), BF16),
        ],
        compiler_params=pltpu.CompilerParams(
            dimension_semantics=("arbitrary", "arbitrary"), vmem_limit_bytes=VMEM_LIMIT),
        name="conv_glu",
    )(x, x, x, mod3, pre_g, w_up, cw, cb, wd, post_g)


def _pad_cols(a, width):
    return jnp.pad(a, ((0, 0), (0, width - a.shape[1])))


def kernel(x, c, ctx, c_ctx, ada_w, ada_b, mix_pre_g, mix_post_g, ffn_pre_g, ffn_post_g, w_in, hg_lb_logits,
           hg_onorm_g, gla_gk_w2, gla_gk_b, gla_onorm_g, w_branch_a, w_branch_b, w_out, ffn_w_up, ffn_conv_w,
           ffn_conv_b, ffn_w_down):
    depth = ada_w.shape[0]
    assert depth == 1, "the context stream update between layers is not implemented"
    layer = 0
    bsz, t, d = x.shape
    assert t % MIX_TILE == 0 and t % FFN_TILE == 0 and FFN_TILE % GRID_W == 0 and MIX_TILE % CHUNK == 0
    assert bsz % MIX_GROUP == 0 and MIX_GROUP == 2

    n_rows = -(-(bsz + 1) // SUBLANES) * SUBLANES
    cc = jnp.concatenate([c, c_ctx[None, :], jnp.zeros((n_rows - bsz - 1, d), F32)], axis=0)
    mod, lb = _mod_call(cc, ada_w[layer], ada_b[layer][None, :], hg_lb_logits, layer)
    mod3 = mod.reshape(n_rows, N_MOD, d)

    sizes = (HG_KEY, HG_KEY, HG_KEY, HG_VAL, HG_VAL, GLA_KEY, GLA_KEY, GLA_VAL, GLA_VAL, GLA_RANK, GLA_RANK, d, d)
    offs = [0]
    for s in sizes:
        offs.append(offs[-1] + s)
    wl = w_in[layer].astype(BF16)
    (w_hq, w_hf_fw, w_hf_bw, w_hi, w_hog, w_gq, w_gk, w_gv, w_gog, w_r_fw, w_r_bw, w_ga, w_gb) = [
        wl[:, offs[i]:offs[i + 1]] for i in range(len(sizes))]
    w_r_fw, w_r_bw = _pad_cols(w_r_fw, LANES), _pad_cols(w_r_bw, LANES)
    w_ctx = jnp.concatenate([w_hf_fw, w_hf_bw, w_hi, w_gk, w_gv, w_r_fw, w_r_bw], axis=1)
    w_scan_bw = jnp.concatenate([w_hq, w_gq, w_gk, w_hi, w_gv, w_hf_bw, w_r_bw], axis=1)
    w_scan_fw = jnp.concatenate([w_hf_fw, w_r_fw], axis=1)
    w_epi = jnp.concatenate([w_hog, w_gog, w_ga, w_gb], axis=1)
    w2p = jnp.pad(gla_gk_w2[layer].astype(BF16), ((0, 0), (0, LANES - GLA_RANK), (0, 0)))
    b2 = gla_gk_b[layer][:, None, :]
    pre_g = mix_pre_g[layer][None, :]
    post_g = mix_post_g[layer][None, :]
    onorm = jnp.concatenate([jnp.tile(hg_onorm_g[layer], HG_HEADS), jnp.tile(gla_onorm_g[layer], GLA_HEADS)])[None, :]

    s_fw, s_bw = _ctx_call(ctx, mod3, pre_g, w_ctx, lb, w2p, b2)

    def grouped(a):
        return a.reshape((MIX_GROUP, bsz // MIX_GROUP) + a.shape[1:])

    xg, modg = grouped(x), grouped(mod3[:bsz])
    o_bw, proj = _bw_call(xg, modg, pre_g, w_scan_bw, lb[1:2], w2p[1], b2[1], grouped(s_bw))
    x1 = _fw_call(xg, modg, pre_g, w_scan_fw, lb[0:1], w2p[0], b2[0], grouped(s_fw), proj, o_bw, w_epi, onorm,
                  w_branch_a[layer].astype(BF16), w_branch_b[layer].astype(BF16), w_out[layer].astype(BF16), post_g)
    x1 = x1.reshape(bsz, t, d)

    f = ffn_w_down.shape[1]
    assert f % FFN_FC == 0
    return _ffn_call(x1, mod3, ffn_pre_g[layer][None, :], ffn_w_up[layer].astype(BF16),
                     ffn_conv_w[layer].reshape(9, f), ffn_conv_b[layer][None, :], ffn_w_down[layer].astype(BF16),
                     ffn_post_g[layer][None, :])
```

```python
import functools

import jax
import jax.numpy as jnp
from jax import lax
from jax.experimental import pallas as pl
from jax.experimental.pallas import tpu as pltpu

F32 = jnp.float32
BF16 = jnp.bfloat16

EPS = 1e-6
GRID_W = 64
CHUNK = 64
N_MOD = 6
HG_HEADS = 4
HG_DK = 128
HG_DV = 128
GLA_HEADS = 4
GLA_DK = 64
GLA_DV = 128
GLA_RANK = 16
GLA_TAU = 16.0
LOG2_E = 1.4426950408889634
LANES = 128
SUBLANES = 8
N_HEADS = HG_HEADS + GLA_HEADS
HG_KEY = HG_HEADS * HG_DK
HG_VAL = HG_HEADS * HG_DV
GLA_KEY = GLA_HEADS * GLA_DK
GLA_VAL = GLA_HEADS * GLA_DV
KEY_W = HG_KEY + GLA_KEY
VAL_W = HG_VAL + GLA_VAL
_P_Q, _P_GK, _P_V, _P_END = 0, KEY_W, KEY_W + GLA_KEY, KEY_W + GLA_KEY + VAL_W
VMEM_LIMIT = 56 * 1024 * 1024

MIX_TILE = 256
MIX_GROUP = 2
BW_TILE = 256
BW_GROUP = 4
FFN_TILE = 512
FFN_FC = 256
EPI_ROWS = 256
SIDE_COLS = 256


def _sigmoid(x):
    return 1.0 / (1.0 + jnp.exp(-x))


def _silu(x):
    return x * _sigmoid(x)


def _log_sigmoid(x):
    return jnp.minimum(x, 0.0) - jnp.log(1.0 + jnp.exp(-jnp.abs(x)))


def _rms(x, g):
    return x * lax.rsqrt(jnp.mean(x * x, axis=-1, keepdims=True) + EPS) * g


def _rms_mod(x, g, shift, scale):
    return (x * lax.rsqrt(jnp.mean(x * x, axis=-1, keepdims=True) + EPS)) * (g * (1.0 + scale)) + shift


def _dot(a, b):
    return jnp.dot(a, b, preferred_element_type=F32)


def _dot_nt(a, b):
    return lax.dot_general(a, b, (((1,), (1,)), ((), ())), preferred_element_type=F32)


def _dot_tn(a, b):
    return lax.dot_general(a, b, (((0,), (0,)), ((), ())), preferred_element_type=F32)


def _cumsum_dot(tri, g):
    g_hi = g.astype(BF16)
    g_lo = (g - g_hi.astype(F32)).astype(BF16)
    return _dot(tri, g_hi) + _dot(tri, g_lo)


def _const_spec(shape):
    nd = len(shape)
    return pl.BlockSpec(shape, lambda *_: (0,) * nd, pipeline_mode=pl.Buffered(1))


def _head_layout(h):
    if h < HG_HEADS:
        return h * HG_DK, None, h * HG_DV
    j = h - HG_HEADS
    return HG_KEY + (j // 2) * LANES, j % 2, HG_VAL + j * GLA_DV


def _lane_mask(sel):
    lane = lax.broadcasted_iota(jnp.int32, (1, LANES), 1)
    return (lane < GLA_DK) if sel == 0 else (lane >= GLA_DK)


def _mod_kernel(layer, cc_ref, w_ref, b_ref, lg_ref, mod_ref, lb_ref):
    s = _silu(cc_ref[...])
    mod_ref[...] = _dot(s.astype(BF16), w_ref[...].astype(BF16)) + b_ref[...]
    n_slot = lg_ref.shape[1]
    lg = [lg_ref[:, j, :] for j in range(n_slot)]
    mx = functools.reduce(jnp.maximum, lg)
    ex = [jnp.exp(l - mx) for l in lg]
    lb_ref[...] = sum(ex[: layer + 1]) / sum(ex)


def _mod_call(cc, ada_w, ada_b, lb_logits, layer):
    d = cc.shape[1]
    n = ada_w.shape[1]
    bn = n // 4
    return pl.pallas_call(
        functools.partial(_mod_kernel, layer),
        grid=(n // bn,),
        in_specs=[
            pl.BlockSpec(cc.shape, lambda j: (0, 0)),
            pl.BlockSpec((d, bn), lambda j: (0, j)),
            pl.BlockSpec((1, bn), lambda j: (0, j)),
            pl.BlockSpec(lb_logits.shape, lambda j: (0, 0, 0)),
        ],
        out_specs=[
            pl.BlockSpec((cc.shape[0], bn), lambda j: (0, j)),
            pl.BlockSpec((lb_logits.shape[0], lb_logits.shape[2]), lambda j: (0, 0)),
        ],
        out_shape=[
            jax.ShapeDtypeStruct((cc.shape[0], n), F32),
            jax.ShapeDtypeStruct((lb_logits.shape[0], lb_logits.shape[2]), F32),
        ],
        compiler_params=pltpu.CompilerParams(dimension_semantics=("arbitrary",), vmem_limit_bytes=VMEM_LIMIT),
        name="mod_vectors",
    )(cc, ada_w, ada_b, lb_logits)


_C_HF_FW, _C_HF_BW, _C_HI, _C_GK, _C_GV, _C_R_FW, _C_R_BW, _C_END = 0, 512, 1024, 1536, 1792, 2304, 2432, 2560


def _ctx_kernel(ctx_ref, mod_ref, pre_g_ref, w_ref, lb_ref, w2_ref, b2_ref, sfw_ref, sbw_ref):
    n = ctx_ref.shape[1]
    mod = mod_ref[0]
    h = _rms_mod(ctx_ref[0], pre_g_ref[...], mod[0:1], mod[1:2])
    z = _dot(h.astype(BF16), w_ref[...])
    v = jnp.concatenate([z[:, _C_HI:_C_GK], z[:, _C_GV:_C_R_FW]], axis=1).astype(BF16)
    gk = z[:, _C_GK:_C_GV]
    row = lax.broadcasted_iota(jnp.int32, (n, n), 0)
    col = lax.broadcasted_iota(jnp.int32, (n, n), 1)
    for d, (c_f, c_r, s_ref) in enumerate(((_C_HF_FW, _C_R_FW, sfw_ref), (_C_HF_BW, _C_R_BW, sbw_ref))):
        lb = lb_ref[d:d + 1, :]
        f = lb + (1.0 - lb) * _sigmoid(z[:, c_f:c_f + HG_KEY])
        r = z[:, c_r:c_r + LANES].astype(BF16)
        g_gla = _log_sigmoid(_dot(r, w2_ref[d]) + b2_ref[d]) / GLA_TAU
        g = jnp.concatenate([jnp.log(f), g_gla], axis=1)
        k = jnp.concatenate([1.0 - f, gk], axis=1)
        tri = ((col > row) if d == 0 else (col < row)).astype(BF16)
        kw = k * jnp.exp(_cumsum_dot(tri, g))
        for hd in range(N_HEADS):
            ko, sel, vo = _head_layout(hd)
            kh = kw[:, ko:ko + LANES]
            if sel is not None:
                kh = jnp.where(_lane_mask(sel), kh, 0.0)
            s_ref[0, hd] = _dot_tn(kh.astype(BF16), v[:, vo:vo + LANES])


def _ctx_call(ctx, mod3, pre_g, w_c, lb, w2p, b2):
    bsz, n, d = ctx.shape
    ctx_row = bsz
    s_shape = jax.ShapeDtypeStruct((bsz, N_HEADS, LANES, LANES), F32)
    s_spec = pl.BlockSpec((1, N_HEADS, LANES, LANES), lambda b: (b, 0, 0, 0))
    return pl.pallas_call(
        _ctx_kernel,
        grid=(bsz,),
        in_specs=[
            pl.BlockSpec((1, n, d), lambda b: (b, 0, 0)),
            pl.BlockSpec((1, N_MOD, d), lambda b: (ctx_row, 0, 0)),
            _const_spec(pre_g.shape),
            _const_spec(w_c.shape),
            _const_spec(lb.shape),
            _const_spec(w2p.shape),
            _const_spec(b2.shape),
        ],
        out_specs=[s_spec, s_spec],
        out_shape=[s_shape, s_shape],
        compiler_params=pltpu.CompilerParams(dimension_semantics=("arbitrary",), vmem_limit_bytes=VMEM_LIMIT),
        name="ctx_states",
    )(ctx, mod3, pre_g, w_c, lb, w2p, b2)


_B_Q, _B_GQ, _B_GK, _B_V, _B_HF, _B_R, _B_END = 0, 512, 768, 1024, 2048, 2560, 2688
_F_HF, _F_R, _F_END = 0, 512, 640
_E_OG, _E_GA, _E_GB, _E_END = 0, 1024, 2048, 3072


def _prefix_rows(x, reverse):
    sub = lax.broadcasted_iota(jnp.int32, (SUBLANES, 1), 0)
    s = 1
    while s < SUBLANES:
        if reverse:
            x = x + jnp.where(sub < SUBLANES - s, pltpu.roll(x, SUBLANES - s, 0), 0.0)
        else:
            x = x + jnp.where(sub >= s, pltpu.roll(x, s, 0), 0.0)
        s *= 2
    return x


def _chunk_cumsum(g_ref, r0, reverse):
    n_grp = CHUNK // SUBLANES
    order = range(n_grp - 1, -1, -1) if reverse else range(n_grp)
    edge = 0 if reverse else SUBLANES - 1
    carry = None
    out = [None] * n_grp
    for i in order:
        p = _prefix_rows(g_ref[pl.ds(r0 + i * SUBLANES, SUBLANES), :], reverse)
        if carry is not None:
            p = p + carry
        carry = p[edge:edge + 1, :]
        out[i] = p
    return jnp.concatenate(out, axis=0), carry


def _round_robin(*stages):
    active = list(stages)
    while active:
        for g in list(active):
            if next(g):
                active.remove(g)


def _scan_stage(reverse, g_ref, hk_ref, p_ref, o_ref, st_s):
    n_chunks = g_ref.shape[0] // CHUNK
    row = lax.broadcasted_iota(jnp.int32, (CHUNK, CHUNK), 0)
    col = lax.broadcasted_iota(jnp.int32, (CHUNK, CHUNK), 1)
    causal = (col >= row) if reverse else (col <= row)

    for i in range(n_chunks):
        c = (n_chunks - 1 - i) if reverse else i
        r0 = c * CHUNK
        rows = slice(r0, r0 + CHUNK)
        b, b_tot = _chunk_cumsum(g_ref, r0, reverse)
        q = p_ref[rows, _P_Q:_P_GK].astype(F32)
        k = jnp.concatenate([hk_ref[rows, :], p_ref[rows, _P_GK:_P_V].astype(F32)], axis=1)
        q_dec = (q * jnp.exp2(b)).astype(BF16)
        k_inv = (k * jnp.exp2(-b)).astype(BF16)
        dec = jnp.exp2(b_tot)
        yield False
        for hd in range(N_HEADS):
            ko, sel, vo = _head_layout(hd)
            qd = q_dec[:, ko:ko + LANES]
            ki = k_inv[:, ko:ko + LANES]
            if sel is not None:
                ki = jnp.where(_lane_mask(sel), ki, jnp.zeros_like(ki))
            vh = p_ref[rows, _P_V + vo:_P_V + vo + LANES]
            att = jnp.where(causal, _dot_nt(qd, ki), 0.0).astype(BF16)
            st = st_s[hd]
            o_ref[rows, vo:vo + LANES] = _dot(jnp.concatenate([qd, att], axis=1),
                                              jnp.concatenate([st.astype(BF16), vh], axis=0))
            dec_col = jnp.broadcast_to(dec[:, ko:ko + LANES], (LANES, LANES)).T
            st_s[hd] = dec_col * (st + _dot_tn(ki, vh))
            if hd % 4 == 3:
                yield i == n_chunks - 1 and hd == N_HEADS - 1


def _decay_pieces(hb, ws_ref, c_hf, c_r, c_end, lb_ref, w2_ref, b2_ref, hk_ref, g_ref):
    lb = lb_ref[...]
    f = lb + (1.0 - lb) * _sigmoid(_dot(hb, ws_ref[:, c_hf:c_r]))
    g_ref[:, 0:HG_KEY] = jnp.log2(f)
    hk_ref[...] = 1.0 - f
    yield False
    r = _dot(hb, ws_ref[:, c_r:c_end]).astype(BF16)
    g_ref[:, HG_KEY:KEY_W] = _log_sigmoid(_dot(r, w2_ref[...]) + b2_ref[...]) * (LOG2_E / GLA_TAU)
    yield False


def _row_views(refs, n_in, tiled_in):
    return [[r.at[gi] if (i in tiled_in or i >= n_in) else r for i, r in enumerate(refs)]
            for gi in range(refs[-1].shape[0])]


def _bw_kernel(*refs):
    st_s, s0_ref = refs[-1], refs[7]

    @pl.when(pl.program_id(1) == 0)
    def _():
        for gi in range(st_s.shape[0]):
            st_s[gi] = s0_ref[gi, 0]

    stages = [_bw_stage(*row) for row in _row_views(refs, 8, (0, 1, 7))]
    for _ in range(3):
        _round_robin(*stages)


def _bw_stage(x_ref, mod_ref, pre_g_ref, ws_ref, lb_ref, w2_ref, b2_ref, _, o_ref, p_ref, h_s, hk_s, g_s, st_s):
    mod = mod_ref[0]
    h = _rms_mod(x_ref[0], pre_g_ref[...], mod[0:1], mod[1:2])
    h_s[...] = h.astype(BF16)
    yield False
    decay = _decay_pieces(h_s[...], ws_ref, _B_HF, _B_R, _B_END, lb_ref, w2_ref, b2_ref, hk_s, g_s)
    yield next(decay)
    next(decay)
    yield True

    for c0 in range(0, _P_END, SIDE_COLS):
        z = _dot(h_s[...], ws_ref[:, c0:c0 + SIDE_COLS])
        if c0 < _P_Q + HG_KEY:
            z = _silu(z) * (HG_DK ** -0.5)
        elif c0 < _P_GK:
            z = z * (GLA_DK ** -0.5)
        p_ref[0, :, c0:c0 + SIDE_COLS] = z.astype(BF16)
        yield c0 + SIDE_COLS == _P_END

    yield from _scan_stage(True, g_s, hk_s, p_ref.at[0], o_ref.at[0], st_s)


def _group(a, g):
    return a.reshape((g, a.shape[0] // g) + a.shape[1:])


def _ungroup(a):
    return a.reshape((a.shape[0] * a.shape[1],) + a.shape[2:])


def _bw_call(x, mod3, pre_g, w_s, lb_d, w2_d, b2_d, s0):
    x, mod3, s0 = _group(x, BW_GROUP), _group(mod3, BW_GROUP), _group(s0, BW_GROUP)
    grp, nb, t, d = x.shape
    tt = BW_TILE
    nt = t // tt
    tile_map = lambda b, j: (0, b, nt - 1 - j, 0)
    consts = (pre_g, w_s, lb_d, w2_d, b2_d)
    o_bw, proj = pl.pallas_call(
        _bw_kernel,
        grid=(nb, nt),
        in_specs=[
            pl.BlockSpec((grp, 1, tt, d), tile_map),
            pl.BlockSpec((grp, 1, N_MOD, d), lambda b, j: (0, b, 0, 0)),
            *[_const_spec(a.shape) for a in consts],
            pl.BlockSpec((grp, 1, N_HEADS, LANES, LANES), lambda b, j: (0, b, 0, 0, 0)),
        ],
        out_specs=[pl.BlockSpec((grp, 1, tt, VAL_W), tile_map), pl.BlockSpec((grp, 1, tt, _P_END), tile_map)],
        out_shape=[jax.ShapeDtypeStruct((grp, nb, t, VAL_W), F32), jax.ShapeDtypeStruct((grp, nb, t, _P_END), BF16)],
        scratch_shapes=[
            pltpu.VMEM((grp, tt, d), BF16),
            pltpu.VMEM((grp, tt, HG_KEY), F32),
            pltpu.VMEM((grp, tt, KEY_W), F32),
            pltpu.VMEM((grp, N_HEADS, LANES, LANES), F32),
        ],
        compiler_params=pltpu.CompilerParams(
            dimension_semantics=("arbitrary", "arbitrary"), vmem_limit_bytes=VMEM_LIMIT),
        name="mixer_bw",
    )(x, mod3, *consts, s0)
    return _ungroup(o_bw), _ungroup(proj)


def _fw_kernel(*refs):
    st_s, s0_ref = refs[-1], refs[7]

    @pl.when(pl.program_id(1) == 0)
    def _():
        for gi in range(st_s.shape[0]):
            st_s[gi] = s0_ref[gi, 0]

    stages = [_fw_stage(*row) for row in _row_views(refs, 16, (0, 1, 7, 8, 9))]
    for _ in range(3):
        _round_robin(*stages)


def _fw_stage(x_ref, mod_ref, pre_g_ref, ws_ref, lb_ref, w2_ref, b2_ref, _, p_ref, obw_ref, we_ref, onorm_ref,
              wa_ref, wb_ref, wo_ref, post_g_ref, out_ref, h_s, hk_s, g_s, o_s, st_s):
    tt = x_ref.shape[1]
    mod = mod_ref[0]
    h = _rms_mod(x_ref[0], pre_g_ref[...], mod[0:1], mod[1:2])
    h_s[...] = h.astype(BF16)
    yield False
    decay = _decay_pieces(h_s[...], ws_ref, _F_HF, _F_R, _F_END, lb_ref, w2_ref, b2_ref, hk_s, g_s)
    yield next(decay)
    next(decay)
    yield True

    yield from _scan_stage(False, g_s, hk_s, p_ref.at[0], o_s, st_s)

    onorm = onorm_ref[...]
    for e in range(tt // EPI_ROWS):
        rows = slice(e * EPI_ROWS, (e + 1) * EPI_ROWS)
        hb = h_s[rows, :]
        o = o_s[rows, :] + obw_ref[0, rows, :]
        og = _dot(hb, we_ref[:, _E_OG:_E_GA])
        ys = []
        for hd in range(N_HEADS):
            lo = hd * LANES
            oh = o[:, lo:lo + LANES]
            yh = oh * lax.rsqrt(jnp.mean(oh * oh, axis=-1, keepdims=True) + EPS) * onorm[:, lo:lo + LANES]
            ys.append((yh * _silu(og[:, lo:lo + LANES])).astype(BF16))
        y_hg = jnp.concatenate(ys[:HG_HEADS], axis=1)
        y_gla = jnp.concatenate(ys[HG_HEADS:], axis=1)
        yield False
        ga = _sigmoid(_dot(hb, we_ref[:, _E_GA:_E_GB]))
        gb = _sigmoid(_dot(hb, we_ref[:, _E_GB:_E_END]))
        yield False
        m = ga * _dot(y_hg, wa_ref[...]) + gb * _dot(y_gla, wb_ref[...])
        yield False
        y = _dot(m.astype(BF16), wo_ref[...])
        out_ref[0, rows, :] = x_ref[0, rows, :] + mod[2:3] * _rms(y, post_g_ref[...])
        yield e == tt // EPI_ROWS - 1


def _fw_call(x, mod3, pre_g, w_s, lb_d, w2_d, b2_d, s0, proj, o_bw, w_epi, onorm, w_a, w_b, w_o, post_g):
    x, mod3, s0, proj, o_bw = (_group(a, MIX_GROUP) for a in (x, mod3, s0, proj, o_bw))
    grp, nb, t, d = x.shape
    tt = MIX_TILE
    nt = t // tt
    tile_map = lambda b, j: (0, b, j, 0)
    consts = (pre_g, w_s, lb_d, w2_d, b2_d)
    consts2 = (w_epi, onorm, w_a, w_b, w_o, post_g)
    return _ungroup(pl.pallas_call(
        _fw_kernel,
        grid=(nb, nt),
        in_specs=[
            pl.BlockSpec((grp, 1, tt, d), tile_map),
            pl.BlockSpec((grp, 1, N_MOD, d), lambda b, j: (0, b, 0, 0)),
            *[_const_spec(a.shape) for a in consts],
            pl.BlockSpec((grp, 1, N_HEADS, LANES, LANES), lambda b, j: (0, b, 0, 0, 0)),
            pl.BlockSpec((grp, 1, tt, _P_END), tile_map),
            pl.BlockSpec((grp, 1, tt, VAL_W), tile_map),
            *[_const_spec(a.shape) for a in consts2],
        ],
        out_specs=pl.BlockSpec((grp, 1, tt, d), tile_map),
        out_shape=jax.ShapeDtypeStruct((grp, nb, t, d), F32),
        scratch_shapes=[
            pltpu.VMEM((grp, tt, d), BF16),
            pltpu.VMEM((grp, tt, HG_KEY), F32),
            pltpu.VMEM((grp, tt, KEY_W), F32),
            pltpu.VMEM((grp, tt, VAL_W), F32),
            pltpu.VMEM((grp, N_HEADS, LANES, LANES), F32),
        ],
        compiler_params=pltpu.CompilerParams(
            dimension_semantics=("arbitrary", "arbitrary"), vmem_limit_bytes=VMEM_LIMIT),
        name="mixer_fw",
    )(x, mod3, *consts, s0, proj, o_bw, *consts2))


def _shift_rows(m, step):
    n = m.shape[0] // SUBLANES
    sub = lax.broadcasted_iota(jnp.int32, (SUBLANES, 1), 0)
    wrapped = (sub == 0) if step == 1 else (sub == SUBLANES - 1)
    rot = [pltpu.roll(m[i * SUBLANES:(i + 1) * SUBLANES, :], step % SUBLANES, 0) for i in range(n)]
    out = []
    for i in range(n):
        j = i - step
        out.append(jnp.where(wrapped, rot[j] if 0 <= j < n else 0.0, rot[i]))
    return jnp.concatenate(out, axis=0)


def _ffn_kernel(x_ref, xp_ref, xn_ref, mod_ref, pre_g_ref, wup_ref, cw_ref, cb_ref, wd_ref, post_g_ref,
                out_ref, h_s, t_s):
    tt = x_ref.shape[1]
    halo = xp_ref.shape[1]
    ext = tt + 2 * halo
    f = wd_ref.shape[0]
    fc = FFN_FC
    n_fc = f // fc
    j = pl.program_id(1)
    mod = mod_ref[0]
    pre_g = pre_g_ref[...]

    def prep(xv):
        return _rms_mod(xv, pre_g, mod[3:4], mod[4:5])

    h_s[0:halo, :] = jnp.where(j > 0, prep(xp_ref[0]), 0.0).astype(BF16)
    h_s[halo:halo + tt, :] = prep(x_ref[0]).astype(BF16)
    h_s[halo + tt:ext, :] = jnp.where(j < pl.num_programs(1) - 1, prep(xn_ref[0]), 0.0).astype(BF16)

    for c in range(n_fc):
        cols = slice(c * fc, (c + 1) * fc)
        u = _dot(h_s[...], wup_ref[:, cols])
        v = _dot(h_s[halo:halo + tt, :], wup_ref[:, f + c * fc:f + (c + 1) * fc])
        mid = [u[e * GRID_W:(e + 1) * GRID_W, :] for e in range(ext // GRID_W)]
        left = [_shift_rows(m, 1) for m in mid]
        right = [_shift_rows(m, -1) for m in mid]
        cw = cw_ref[:, cols]
        cb = cb_ref[:, cols]
        for r in range(tt // GRID_W):
            acc = None
            for dr in range(3):
                for dc, src in enumerate((left, mid, right)):
                    term = src[r + dr] * cw[dr * 3 + dc:dr * 3 + dc + 1, :]
                    acc = term + cb if acc is None else acc + term
            inner = acc * (0.7978845608028654 + (0.7978845608028654 * 0.044715) * (acc * acc))
            half = 0.5 * acc
            rows = slice(r * GRID_W, (r + 1) * GRID_W)
            t_s[c, rows, :] = ((half + half * jnp.tanh(inner)) * v[rows, :]).astype(BF16)

    y = _dot(t_s[0], wd_ref[0:fc, :])
    for c in range(1, n_fc):
        y = y + _dot(t_s[c], wd_ref[c * fc:(c + 1) * fc, :])
    out_ref[0] = x_ref[0] + mod[5:6] * _rms(y, post_g_ref[...])


def _ffn_call(x, mod3, pre_g, w_up, cw, cb, wd, post_g):
    bsz, t, d = x.shape
    tt = FFN_TILE
    nt = t // tt
    halo = GRID_W
    per = tt // halo
    n_halo = t // halo
    n_fc = wd.shape[0] // FFN_FC
    return pl.pallas_call(
        _ffn_kernel,
        grid=(bsz, nt),
        in_specs=[
            pl.BlockSpec((1, tt, d), lambda b, j: (b, j, 0)),
            pl.BlockSpec((1, halo, d), lambda b, j: (b, jnp.maximum(j * per - 1, 0), 0)),
            pl.BlockSpec((1, halo, d), lambda b, j: (b, jnp.minimum((j + 1) * per, n_halo - 1), 0)),
            pl.BlockSpec((1, N_MOD, d), lambda b, j: (b, 0, 0)),
            _const_spec(pre_g.shape),
            _const_spec(w_up.shape),
            _const_spec(cw.shape),
            _const_spec(cb.shape),
            _const_spec(wd.shape),
            _const_spec(post_g.shape),
        ],
        out_specs=pl.BlockSpec((1, tt, d), lambda b, j: (b, j, 0)),
        out_shape=jax.ShapeDtypeStruct((bsz, t, d), F32),
        scratch_shapes=[
            pltpu.VMEM((tt + 2 * halo, d), BF16),
            pltpu.VMEM((n_fc, tt, FFN_FC), BF16),
        ],
        compiler_params=pltpu.CompilerParams(
            dimension_semantics=("arbitrary", "arbitrary"), vmem_limit_bytes=VMEM_LIMIT),
        name="conv_glu",
    )(x, x, x, mod3, pre_g, w_up, cw, cb, wd, post_g)


def _pad_cols(a, width):
    return jnp.pad(a, ((0, 0), (0, width - a.shape[1])))


def kernel(x, c, ctx, c_ctx, ada_w, ada_b, mix_pre_g, mix_post_g, ffn_pre_g, ffn_post_g, w_in, hg_lb_logits,
           hg_onorm_g, gla_gk_w2, gla_gk_b, gla_onorm_g, w_branch_a, w_branch_b, w_out, ffn_w_up, ffn_conv_w,
           ffn_conv_b, ffn_w_down):
    depth = ada_w.shape[0]
    assert depth == 1, "the context stream update between layers is not implemented"
    layer = 0
    bsz, t, d = x.shape
    assert t % MIX_TILE == 0 and t % FFN_TILE == 0 and FFN_TILE % GRID_W == 0 and MIX_TILE % CHUNK == 0
    assert bsz % MIX_GROUP == 0 and bsz % BW_GROUP == 0 and t % BW_TILE == 0 and BW_TILE % CHUNK == 0

    n_rows = -(-(bsz + 1) // SUBLANES) * SUBLANES
    cc = jnp.concatenate([c, c_ctx[None, :], jnp.zeros((n_rows - bsz - 1, d), F32)], axis=0)
    mod, lb = _mod_call(cc, ada_w[layer], ada_b[layer][None, :], hg_lb_logits, layer)
    mod3 = mod.reshape(n_rows, N_MOD, d)

    sizes = (HG_KEY, HG_KEY, HG_KEY, HG_VAL, HG_VAL, GLA_KEY, GLA_KEY, GLA_VAL, GLA_VAL, GLA_RANK, GLA_RANK, d, d)
    offs = [0]
    for s in sizes:
        offs.append(offs[-1] + s)
    wl = w_in[layer].astype(BF16)
    (w_hq, w_hf_fw, w_hf_bw, w_hi, w_hog, w_gq, w_gk, w_gv, w_gog, w_r_fw, w_r_bw, w_ga, w_gb) = [
        wl[:, offs[i]:offs[i + 1]] for i in range(len(sizes))]
    w_r_fw, w_r_bw = _pad_cols(w_r_fw, LANES), _pad_cols(w_r_bw, LANES)
    w_ctx = jnp.concatenate([w_hf_fw, w_hf_bw, w_hi, w_gk, w_gv, w_r_fw, w_r_bw], axis=1)
    w_scan_bw = jnp.concatenate([w_hq, w_gq, w_gk, w_hi, w_gv, w_hf_bw, w_r_bw], axis=1)
    w_scan_fw = jnp.concatenate([w_hf_fw, w_r_fw], axis=1)
    w_epi = jnp.concatenate([w_hog, w_gog, w_ga, w_gb], axis=1)
    w2p = jnp.pad(gla_gk_w2[layer].astype(BF16), ((0, 0), (0, LANES - GLA_RANK), (0, 0)))
    b2 = gla_gk_b[layer][:, None, :]
    pre_g = mix_pre_g[layer][None, :]
    post_g = mix_post_g[layer][None, :]
    onorm = jnp.concatenate([jnp.tile(hg_onorm_g[layer], HG_HEADS), jnp.tile(gla_onorm_g[layer], GLA_HEADS)])[None, :]

    s_fw, s_bw = _ctx_call(ctx, mod3, pre_g, w_ctx, lb, w2p, b2)

    o_bw, proj = _bw_call(x, mod3[:bsz], pre_g, w_scan_bw, lb[1:2], w2p[1], b2[1], s_bw)
    x1 = _fw_call(x, mod3[:bsz], pre_g, w_scan_fw, lb[0:1], w2p[0], b2[0], s_fw, proj, o_bw, w_epi, onorm,
                  w_branch_a[layer].astype(BF16), w_branch_b[layer].astype(BF16), w_out[layer].astype(BF16), post_g)

    f = ffn_w_down.shape[1]
    assert f % FFN_FC == 0
    return _ffn_call(x1, mod3, ffn_pre_g[layer][None, :], ffn_w_up[layer].astype(BF16),
                     ffn_conv_w[layer].reshape(9, f), ffn_conv_b[layer][None, :], ffn_w_down[layer].astype(BF16),
                     ffn_post_g[layer][None, :])
```
